```python
import math
import jax, jax.numpy as jnp
from jax import lax
import numpy as np

D_MODEL = 1024
BATCH = 16
SEQ = 2048
DEPTH = 4

CTX_LEN = 256
GRID_W = 64
NORM_EPS = 1e-6
HEAD_DIM = 64

ATTN_WIDTH = D_MODEL // 2
ATTN_HEADS = ATTN_WIDTH // HEAD_DIM
ATTN_KV_HEADS = 2
GQA_GROUP = ATTN_HEADS // ATTN_KV_HEADS
ATTN_KV_WIDTH = ATTN_KV_HEADS * HEAD_DIM
ROPE_THETA = 10000.0
Q_BLOCK = 128

GDN_WIDTH = D_MODEL // 4
GDN_DK = 64
GDN_DV = 64
GDN_HEADS = GDN_WIDTH // GDN_DV
GDN_KEY_WIDTH = GDN_HEADS * GDN_DK
GDN_QKV_WIDTH = 2 * GDN_KEY_WIDTH + GDN_WIDTH
GDN_CHUNK = 64
SHORT_CONV = 3

HYENA_WIDTH = D_MODEL // 4
HYENA_ORDER = 2
HYENA_EMB_BANDS = 16
HYENA_EMB_DIM = 1 + 2 * HYENA_EMB_BANDS
HYENA_FILTER_HIDDEN = 64
HYENA_FAST_DECAY = 0.3
HYENA_SLOW_DECAY = 1.5
HYENA_DECAY_TARGET = 1e-2
HYENA_CONV = 3

MIX_WIDTH = ATTN_WIDTH + GDN_WIDTH + HYENA_WIDTH
IN_SPLITS = (ATTN_WIDTH, ATTN_KV_WIDTH, ATTN_KV_WIDTH,
             GDN_QKV_WIDTH, GDN_WIDTH, 2 * GDN_HEADS, 2 * GDN_HEADS,
             (HYENA_ORDER + 1) * HYENA_WIDTH)
IN_WIDTH = sum(IN_SPLITS)

D_FF = 2816
FFN_CONV = 3

kernel_name = 'hybrid_gqa_gdn_hyena_prefix_dit'


def rms_norm(x, gain):
    xf = x.astype(jnp.float32)
    y = xf * lax.rsqrt(jnp.mean(xf * xf, axis=-1, keepdims=True) + NORM_EPS)
    return (y * gain.astype(jnp.float32)).astype(x.dtype)


def l2_normalize(x):
    xf = x.astype(jnp.float32)
    return xf * lax.rsqrt(jnp.sum(xf * xf, axis=-1, keepdims=True) + NORM_EPS)


def dw_conv(x, w):
    return lax.conv_general_dilated(x, w[:, None, :].astype(x.dtype), window_strides=(1,), padding='SAME',
                                    dimension_numbers=('NWC', 'WIO', 'NWC'), feature_group_count=x.shape[-1])


def rope_1d(x, pos):
    d = x.shape[-1]
    inv = ROPE_THETA ** (-jnp.arange(d // 2, dtype=jnp.float32) / (d // 2))
    ang = pos.astype(jnp.float32)[:, None] * inv[None, :]
    cos = jnp.cos(ang)[None, :, None, :]
    sin = jnp.sin(ang)[None, :, None, :]
    x1, x2 = x[..., :d // 2], x[..., d // 2:]
    return jnp.concatenate([x1 * cos - x2 * sin, x2 * cos + x1 * sin], axis=-1)


def axial_rope(x):
    n_tok = x.shape[1]
    rows = n_tok // GRID_W
    row_id = jnp.repeat(jnp.arange(rows), GRID_W)
    col_id = jnp.tile(jnp.arange(GRID_W), rows)
    half = x.shape[-1] // 2
    xf = x.astype(jnp.float32)
    out = jnp.concatenate([rope_1d(xf[..., :half], row_id), rope_1d(xf[..., half:], col_id)], axis=-1)
    return out.astype(x.dtype)


def gqa_attend(q, k, v):
    s = jnp.einsum('bqkgd,bskd->bkgqs', q, k).astype(jnp.float32) * (HEAD_DIM ** -0.5)
    p = jax.nn.softmax(s, axis=-1).astype(v.dtype)
    return jnp.einsum('bkgqs,bskd->bqkgd', p, v)


def attention_mixer(qkv_c, qkv_l, q_gain, k_gain, with_ctx):
    def heads(t, n):
        return t.reshape(t.shape[0], t.shape[1], n, HEAD_DIM)
    q_c, k_c, v_c = qkv_c
    q_l, k_l, v_l = qkv_l
    b, n_lat = q_l.shape[:2]
    kc = rms_norm(heads(k_c, ATTN_KV_HEADS), k_gain)
    vc = heads(v_c, ATTN_KV_HEADS)
    ql = axial_rope(rms_norm(heads(q_l, ATTN_HEADS), q_gain))
    kl = axial_rope(rms_norm(heads(k_l, ATTN_KV_HEADS), k_gain))
    vl = heads(v_l, ATTN_KV_HEADS)
    k_all = jnp.concatenate([kc, kl], axis=1)
    v_all = jnp.concatenate([vc, vl], axis=1)
    qb = ql.reshape(b, n_lat // Q_BLOCK, Q_BLOCK, ATTN_KV_HEADS, GQA_GROUP, HEAD_DIM)
    out_l = lax.map(lambda q: gqa_attend(q, k_all, v_all), jnp.moveaxis(qb, 1, 0))
    out_l = jnp.moveaxis(out_l, 0, 1).reshape(b, n_lat, ATTN_WIDTH)
    out_c = None
    if with_ctx:
        n_ctx = q_c.shape[1]
        qc = rms_norm(heads(q_c, ATTN_HEADS), q_gain).reshape(b, n_ctx, ATTN_KV_HEADS, GQA_GROUP, HEAD_DIM)
        out_c = gqa_attend(qc, kc, vc).reshape(b, n_ctx, ATTN_WIDTH)
    return out_c, out_l


def gated_delta_chunked(q, k, v, g, beta, state0):
    b, n_tok, h, dk = k.shape
    dv = v.shape[-1]
    c = GDN_CHUNK
    n = n_tok // c

    def to_chunks(t):
        t = t.astype(jnp.float32).reshape((b, n, c) + t.shape[2:])
        return jnp.moveaxis(t, 3, 2)
    q = to_chunks(q) * (dk ** -0.5)
    k, v, g, beta = (to_chunks(t) for t in (k, v, g, beta))
    g = jnp.cumsum(g, axis=-1)
    lower = jnp.tril(jnp.ones((c, c), dtype=bool))
    diff = g[..., :, None] - g[..., None, :]
    decay = jnp.where(lower, jnp.exp(jnp.where(lower, diff, 0.0)), 0.0)
    k_beta = k * beta[..., None]
    kk = jnp.einsum('bnhcd,bnhsd->bnhcs', k_beta, k) * decay
    t_mat = jnp.eye(c, dtype=jnp.float32) + jnp.tril(kk, -1)
    rhs = jnp.concatenate([v * beta[..., None], k_beta * jnp.exp(g)[..., None]], axis=-1)
    sol = lax.linalg.triangular_solve(t_mat, rhs, left_side=True, lower=True, unit_diagonal=True)
    u, w = sol[..., :dv], sol[..., dv:]
    qk = jnp.einsum('bnhcd,bnhsd->bnhcs', q, k) * decay
    q_dec = q * jnp.exp(g)[..., None]
    k_dec = k * jnp.exp(g[..., -1:] - g)[..., None]
    g_end = jnp.exp(g[..., -1])

    def step(state, xs):
        u_i, w_i, qk_i, qd_i, kd_i, ge_i = xs
        v_new = u_i - jnp.einsum('bhcd,bhde->bhce', w_i, state)
        o = jnp.einsum('bhcd,bhde->bhce', qd_i, state) + jnp.einsum('bhcs,bhse->bhce', qk_i, v_new)
        state = state * ge_i[..., None, None] + jnp.einsum('bhcd,bhce->bhde', kd_i, v_new)
        return state, o
    xs = tuple(jnp.moveaxis(t, 1, 0) for t in (u, w, qk, q_dec, k_dec, g_end))
    state, o = lax.scan(step, state0.astype(jnp.float32), xs)
    o = jnp.moveaxis(o, 0, 1)
    o = jnp.moveaxis(o, 2, 3).reshape(b, n_tok, h, dv)
    return o, state


def gdn_mixer(p_c, p_l, conv_w, a_log, dt_bias, norm_gain, with_ctx):
    def prep(qkv, beta_logit, decay_logit):
        b, n, _ = qkv.shape
        qkv = jax.nn.silu(dw_conv(qkv, conv_w))
        q, k, v = jnp.split(qkv, [GDN_KEY_WIDTH, 2 * GDN_KEY_WIDTH], axis=-1)
        q = l2_normalize(q.reshape(b, n, GDN_HEADS, GDN_DK))
        k = l2_normalize(k.reshape(b, n, GDN_HEADS, GDN_DK))
        v = v.reshape(b, n, GDN_HEADS, GDN_DV)
        beta = jax.nn.sigmoid(beta_logit.astype(jnp.float32)).reshape(b, n, 2, GDN_HEADS)
        g = -jnp.exp(a_log.astype(jnp.float32)) * jax.nn.softplus(
            decay_logit.astype(jnp.float32).reshape(b, n, 2, GDN_HEADS) + dt_bias.astype(jnp.float32))
        return q, k, v, g, beta
    qkv_c, gate_c, beta_c, dec_c = p_c
    qkv_l, gate_l, beta_l, dec_l = p_l
    ctx_in = prep(qkv_c, beta_c, dec_c)
    lat_in = prep(qkv_l, beta_l, dec_l)
    b = qkv_l.shape[0]
    outs_c, outs_l = [], []
    for d in range(2):
        orient = (lambda t: jnp.flip(t, axis=1)) if d == 1 else (lambda t: t)

        def dir_inputs(inp):
            q, k, v, g, beta = inp
            return orient(q), orient(k), orient(v), orient(g[:, :, d]), orient(beta[:, :, d])
        s0 = jnp.zeros((b, GDN_HEADS, GDN_DK, GDN_DV), jnp.float32)
        o_c, s_ctx = gated_delta_chunked(*dir_inputs(ctx_in), s0)
        o_l, _ = gated_delta_chunked(*dir_inputs(lat_in), s_ctx)
        outs_c.append(orient(o_c))
        outs_l.append(orient(o_l))

    def finish(o, gate):
        bb, n = gate.shape[:2]
        y = rms_norm(o, norm_gain) * jax.nn.silu(gate.astype(jnp.float32)).reshape(bb, n, GDN_HEADS, GDN_DV)
        return y.reshape(bb, n, GDN_WIDTH).astype(gate.dtype)
    out_l = finish(outs_l[0] + outs_l[1], gate_l)
    out_c = finish(outs_c[0] + outs_c[1], gate_c) if with_ctx else None
    return out_c, out_l


def hyena_filter_bank(n_tok, w1, b1, w2, b2, w3, freq):
    f32 = jnp.float32
    t = jnp.linspace(0.0, 1.0, n_tok, dtype=f32)[:, None]
    omega = 2.0 * math.pi * jnp.arange(n_tok, dtype=f32)[:, None] / n_tok
    bands = jnp.linspace(1e-4, HYENA_EMB_BANDS - 1, HYENA_EMB_BANDS, dtype=f32)[None, :]
    feats = jnp.concatenate([t, jnp.cos(bands * omega), -jnp.sin(bands * omega)], axis=-1)
    fr = freq.astype(f32)
    hid = jnp.sin(fr * (feats @ w1.astype(f32) + b1.astype(f32)))
    hid = jnp.sin(fr * (hid @ w2.astype(f32) + b2.astype(f32)))
    filt = (hid @ w3.astype(f32)).reshape(n_tok, HYENA_ORDER - 1, 2, HYENA_WIDTH)
    max_decay = math.log(HYENA_DECAY_TARGET) / HYENA_FAST_DECAY
    min_decay = math.log(HYENA_DECAY_TARGET) / HYENA_SLOW_DECAY
    deltas = jnp.abs(jnp.linspace(min_decay, max_decay, HYENA_WIDTH, dtype=f32))
    window = jnp.exp(-t * deltas[None, :])
    return filt * window[:, None, None, :]


def long_conv(u, h_fwd, h_bwd, bias):
    n_tok, width = h_fwd.shape
    n_fft = 2 * n_tok
    h = jnp.concatenate([h_fwd, jnp.zeros((1, width), jnp.float32), h_bwd[:0:-1]], axis=0)
    uf = u.astype(jnp.float32)
    y = jnp.fft.irfft(jnp.fft.rfft(uf, n=n_fft, axis=1) * jnp.fft.rfft(h, axis=0)[None], n=n_fft, axis=1)[:, :n_tok]
    return (y + uf * bias.astype(jnp.float32)).astype(u.dtype)


def hyena_mixer(p, conv_w, filt, bias):
    streams = jnp.split(dw_conv(p, conv_w), HYENA_ORDER + 1, axis=-1)
    gates, v = streams[:-1], streams[-1]
    for o, gate in enumerate(reversed(gates[1:])):
        v = long_conv(v * gate, filt[:, o, 0], filt[:, o, 1], bias[o])
    return v * gates[0]


def split_cols(p):
    idx = [int(i) for i in np.cumsum(IN_SPLITS)[:-1]]
    return jnp.split(p, idx, axis=-1)


def token_mixers(p_c, p_l, q_gain, k_gain, gdn_conv, gdn_a_log, gdn_dt_bias, gdn_norm,
                 hyena_conv, hw1, hb1, hw2, hb2, hw3, hfreq, hbias, with_ctx):
    cc = split_cols(p_c)
    cl = split_cols(p_l)
    a_c, a_l = attention_mixer(cc[0:3], cl[0:3], q_gain, k_gain, with_ctx)
    g_c, g_l = gdn_mixer(cc[3:7], cl[3:7], gdn_conv, gdn_a_log, gdn_dt_bias, gdn_norm, with_ctx)
    filt_l = hyena_filter_bank(p_l.shape[1], hw1, hb1, hw2, hb2, hw3, hfreq)
    y_l = hyena_mixer(cl[7], hyena_conv, filt_l, hbias)
    mix_l = jnp.concatenate([a_l, g_l, y_l], axis=-1)
    mix_c = None
    if with_ctx:
        filt_c = hyena_filter_bank(p_c.shape[1], hw1, hb1, hw2, hb2, hw3, hfreq)
        y_c = hyena_mixer(cc[7], hyena_conv, filt_c, hbias)
        mix_c = jnp.concatenate([a_c, g_c, y_c], axis=-1)
    return mix_c, mix_l


def conv_ffn(u, w_up, conv_w, w_down):
    gate, val = jnp.split(dw_conv(u @ w_up, conv_w), 2, axis=-1)
    return (jax.nn.silu(gate) * val) @ w_down


def adaln(cvec, w_ada, b_ada):
    mod = jax.nn.silu(cvec) @ w_ada + b_ada
    return [m[:, None, :] for m in jnp.split(mod, 6, axis=-1)]


def setup_inputs(seed: int = 0) -> dict:
    key = jax.random.key(seed)
    ks = jax.random.split(key, 28)
    f32 = jnp.float32

    def nrm(k, shape, scale):
        return jax.random.normal(k, shape, f32) * scale

    def gain(k, shape):
        return 1.0 + 0.01 * jax.random.normal(k, shape, f32)
    dt = jnp.exp(jax.random.uniform(ks[14], (DEPTH, 2, GDN_HEADS), f32, math.log(1e-3), math.log(1e-1)))
    return {
        'x': nrm(ks[0], (BATCH, SEQ, D_MODEL), 1.0),
        'c': nrm(ks[1], (BATCH, D_MODEL), 1.0),
        'ctx': nrm(ks[2], (BATCH, CTX_LEN, D_MODEL), 1.0),
        'c_ctx': nrm(ks[3], (D_MODEL,), 1.0),
        'w_ada': nrm(ks[4], (DEPTH, D_MODEL, 6 * D_MODEL), D_MODEL ** -0.5),
        'b_ada': nrm(ks[5], (DEPTH, 6 * D_MODEL), 0.01),
        'norm1': gain(ks[6], (DEPTH, D_MODEL)),
        'norm2': gain(ks[7], (DEPTH, D_MODEL)),
        'w_in': nrm(ks[8], (DEPTH, D_MODEL, IN_WIDTH), D_MODEL ** -0.5),
        'w_out': nrm(ks[9], (DEPTH, MIX_WIDTH, D_MODEL), MIX_WIDTH ** -0.5),
        'q_gain': gain(ks[10], (DEPTH, HEAD_DIM)),
        'k_gain': gain(ks[11], (DEPTH, HEAD_DIM)),
        'gdn_conv': nrm(ks[12], (DEPTH, SHORT_CONV, GDN_QKV_WIDTH), SHORT_CONV ** -0.5),
        'gdn_a_log': jnp.log(jax.random.uniform(ks[13], (DEPTH, 2, GDN_HEADS), f32, 1.0, 16.0)),
        'gdn_dt_bias': dt + jnp.log(-jnp.expm1(-dt)),
        'gdn_norm': gain(ks[15], (DEPTH, GDN_DV)),
        'hyena_conv': nrm(ks[16], (DEPTH, HYENA_CONV, (HYENA_ORDER + 1) * HYENA_WIDTH), HYENA_CONV ** -0.5),
        'hyena_w1': nrm(ks[17], (DEPTH, HYENA_EMB_DIM, HYENA_FILTER_HIDDEN), HYENA_EMB_DIM ** -0.5),
        'hyena_b1': nrm(ks[18], (DEPTH, HYENA_FILTER_HIDDEN), 0.02),
        'hyena_w2': nrm(ks[19], (DEPTH, HYENA_FILTER_HIDDEN, HYENA_FILTER_HIDDEN), HYENA_FILTER_HIDDEN ** -0.5),
        'hyena_b2': nrm(ks[20], (DEPTH, HYENA_FILTER_HIDDEN), 0.02),
        'hyena_w3': nrm(ks[21], (DEPTH, HYENA_FILTER_HIDDEN, (HYENA_ORDER - 1) * 2 * HYENA_WIDTH),
                        0.1 * HYENA_FILTER_HIDDEN ** -0.5),
        'hyena_freq': gain(ks[22], (DEPTH, HYENA_FILTER_HIDDEN)),
        'hyena_bias': nrm(ks[23], (DEPTH, HYENA_ORDER - 1, HYENA_WIDTH), 1.0),
        'ffn_up': nrm(ks[24], (DEPTH, D_MODEL, 2 * D_FF), D_MODEL ** -0.5),
        'ffn_conv': nrm(ks[25], (DEPTH, FFN_CONV, 2 * D_FF), FFN_CONV ** -0.5),
        'ffn_down': nrm(ks[26], (DEPTH, D_FF, D_MODEL), D_FF ** -0.5),
        'final_norm': gain(ks[27], (D_MODEL,)),
    }


def reference(x, c, ctx, c_ctx, w_ada, b_ada, norm1, norm2, w_in, w_out, q_gain, k_gain,
              gdn_conv, gdn_a_log, gdn_dt_bias, gdn_norm, hyena_conv, hyena_w1, hyena_b1, hyena_w2,
              hyena_b2, hyena_w3, hyena_freq, hyena_bias, ffn_up, ffn_conv, ffn_down, final_norm):
    h, hc = x, ctx
    for i in range(DEPTH):
        with_ctx = i < DEPTH - 1
        sh1, sc1, g1, sh2, sc2, g2 = adaln(c, w_ada[i], b_ada[i])
        csh1, csc1, cg1, csh2, csc2, cg2 = adaln(c_ctx[None, :], w_ada[i], b_ada[i])
        u_l = rms_norm(h, norm1[i]) * (1 + sc1) + sh1
        u_c = rms_norm(hc, norm1[i]) * (1 + csc1) + csh1
        mix_c, mix_l = token_mixers(u_c @ w_in[i], u_l @ w_in[i], q_gain[i], k_gain[i],
                                    gdn_conv[i], gdn_a_log[i], gdn_dt_bias[i], gdn_norm[i],
                                    hyena_conv[i], hyena_w1[i], hyena_b1[i], hyena_w2[i], hyena_b2[i],
                                    hyena_w3[i], hyena_freq[i], hyena_bias[i], with_ctx)
        h = h + g1 * (mix_l @ w_out[i])
        h = h + g2 * conv_ffn(rms_norm(h, norm2[i]) * (1 + sc2) + sh2, ffn_up[i], ffn_conv[i], ffn_down[i])
        if with_ctx:
            hc = hc + cg1 * (mix_c @ w_out[i])
            hc = hc + cg2 * conv_ffn(rms_norm(hc, norm2[i]) * (1 + csc2) + csh2, ffn_up[i], ffn_conv[i], ffn_down[i])
    return rms_norm(h, final_norm)
```

```python
import functools
import math

import jax
import jax.numpy as jnp
from jax import lax
from jax.experimental import pallas as pl
from jax.experimental.pallas import tpu as pltpu

F32 = jnp.float32
BF16 = jnp.bfloat16

D_MODEL = 1024
BATCH = 16
SEQ = 2048
DEPTH = 4
CTX_LEN = 256
STREAM = SEQ + CTX_LEN
TILE = 256
N_TILES = STREAM // TILE
N_LAT_TILES = SEQ // TILE
CTX_TILE = N_LAT_TILES
GRID_W = 64
NORM_EPS = 1e-6
HEAD_DIM = 64
LANES = 128
MOD_ROWS = 24
CTX_MOD_ROW = BATCH

ATTN_WIDTH = 512
ATTN_HEADS = 8
ATTN_KV_WIDTH = 128
ROPE_THETA = 10000.0

GDN_WIDTH = 256
GDN_HEADS = 4
GDN_DK = 64
GDN_QKV_WIDTH = 768
GDN_CHUNK = 64
N_CHUNKS = STREAM // GDN_CHUNK
N_LAT_CHUNKS = SEQ // GDN_CHUNK
N_CTX_CHUNKS = CTX_LEN // GDN_CHUNK

HYENA_WIDTH = 256
HYENA_EMB_BANDS = 16
HYENA_FILTER_HIDDEN = 64
HYENA_FAST_DECAY = 0.3
HYENA_SLOW_DECAY = 1.5
HYENA_DECAY_TARGET = 1e-2

D_FF = 2816
FF_CHUNK = 256
N_FF_CHUNKS = D_FF // FF_CHUNK
HALO = 8

COL_ATTN = (0, 768)
COL_GQKV = (768, 1536)
COL_GGATE = (1536, 1792)
COL_HY = (1792, 2560)
COL_BD = (2560, 2688)
IN_COLS = 2688

VMEM_LIMIT = 56 * 1024 * 1024


def _dot(a, b):
    return jnp.dot(a, b, preferred_element_type=F32)


def _dot_nt(a, b):
    return lax.dot_general(a, b, (((1,), (1,)), ((), ())), preferred_element_type=F32)


def _dot_tn(a, b):
    return lax.dot_general(a, b, (((0,), (0,)), ((), ())), preferred_element_type=F32)


def _sigmoid(x):
    return 1.0 / (1.0 + jnp.exp(-x))


def _silu(x):
    return x * _sigmoid(x)


def _softplus(x):
    return jnp.maximum(x, 0.0) + jnp.log(1.0 + jnp.exp(-jnp.abs(x)))


def _seg_sum(x, seg):
    hi = x.astype(BF16)
    lo = (x - hi.astype(F32)).astype(BF16)
    return _dot(hi, seg) + _dot(lo, seg)


def _tri_dot(tri, x):
    x1 = x.astype(BF16)
    r1 = x - x1.astype(F32)
    x2 = r1.astype(BF16)
    x3 = (r1 - x2.astype(F32)).astype(BF16)
    return _dot(tri, x1) + _dot(tri, x2) + _dot(tri, x3)


def _params(*sem):
    return pltpu.CompilerParams(dimension_semantics=sem, vmem_limit_bytes=VMEM_LIMIT)


def _const_spec(shape):
    nd = len(shape)
    return pl.BlockSpec(shape, lambda *_: (0,) * nd, pipeline_mode=pl.Buffered(1))


def _mod_row(b, t):
    return jnp.where(t == CTX_TILE, CTX_MOD_ROW, b)


def _ada_kernel(c_ref, w_ref, b_ref, o_ref):
    s = _silu(c_ref[...]).astype(BF16)
    o_ref[...] = _dot(s, w_ref[...].astype(BF16)) + b_ref[...]


def _ada_call(cvec, w_ada, b_ada):
    nj = 6
    return pl.pallas_call(
        _ada_kernel,
        grid=(DEPTH, nj),
        in_specs=[
            pl.BlockSpec((MOD_ROWS, D_MODEL), lambda i, j: (0, 0)),
            pl.BlockSpec((None, D_MODEL, D_MODEL), lambda i, j: (i, 0, j)),
            pl.BlockSpec((None, 1, D_MODEL), lambda i, j: (i, 0, j)),
        ],
        out_specs=pl.BlockSpec((None, MOD_ROWS, D_MODEL), lambda i, j: (i, 0, j)),
        out_shape=jax.ShapeDtypeStruct((DEPTH, MOD_ROWS, 6 * D_MODEL), F32),
        compiler_params=_params("arbitrary", "arbitrary"),
        name="adaln",
    )(cvec, w_ada, b_ada.reshape(DEPTH, 1, 6 * D_MODEL))


def _norm_mod(x, gain, shift, scale):
    ms = jnp.mean(x * x, axis=-1, keepdims=True)
    return (x * lax.rsqrt(ms + NORM_EPS) * gain) * (1.0 + scale) + shift


def _in_kernel(x_ref, m_ref, g_ref, w_ref, oa_ref, ogq_ref, ogg_ref, ohy_ref, obd_ref):
    u = _norm_mod(x_ref[...], g_ref[...], m_ref[0:1, :], m_ref[1:2, :]).astype(BF16)
    oa_ref[...] = _dot(u, w_ref[:, COL_ATTN[0]:COL_ATTN[1]]).astype(BF16)
    ogq_ref[...] = _dot(u, w_ref[:, COL_GQKV[0]:COL_GQKV[1]]).astype(BF16)
    ogg_ref[...] = _dot(u, w_ref[:, COL_GGATE[0]:COL_GGATE[1]]).astype(BF16)
    ohy_ref[...] = _dot(u, w_ref[:, COL_HY[0]:COL_HY[1]]).astype(BF16)
    obd_ref[...] = _dot(u, w_ref[:, COL_BD[0]:COL_BD[1]])


def _in_call(h, mod, gain, w_cat):
    def tile(width):
        return pl.BlockSpec((None, TILE, width), lambda b, t: (b, t, 0))
    return pl.pallas_call(
        _in_kernel,
        grid=(BATCH, N_TILES),
        in_specs=[
            tile(D_MODEL),
            pl.BlockSpec((None, 6, D_MODEL), lambda b, t: (_mod_row(b, t), 0, 0)),
            _const_spec((1, D_MODEL)),
            _const_spec((D_MODEL, IN_COLS)),
        ],
        out_specs=[tile(768), tile(768), tile(256), tile(768), tile(LANES)],
        out_shape=[
            jax.ShapeDtypeStruct((BATCH, STREAM, 768), BF16),
            jax.ShapeDtypeStruct((BATCH, STREAM, 768), BF16),
            jax.ShapeDtypeStruct((BATCH, STREAM, 256), BF16),
            jax.ShapeDtypeStruct((BATCH, STREAM, 768), BF16),
            jax.ShapeDtypeStruct((BATCH, STREAM, LANES), F32),
        ],
        compiler_params=_params("arbitrary", "arbitrary"),
        name="in_proj",
    )(h, mod, gain, w_cat)


def _attn_prep_kernel(a_ref, cos_ref, sin_ref, qg_ref, kg_ref, seg_ref, q_ref, kt_ref, v_ref):
    lane = lax.broadcasted_iota(jnp.int32, (TILE, LANES), 1)
    first = (lane % 32) < 16
    low = lane < HEAD_DIM
    seg = seg_ref[...]
    cosv = cos_ref[...]
    sinv = sin_ref[...]

    def norm_rope(x, gain):
        ms = _seg_sum(x * x, seg) * (1.0 / HEAD_DIM)
        y = x * lax.rsqrt(ms + NORM_EPS) * gain
        partner = jnp.where(first, pltpu.roll(y, LANES - 16, 1), pltpu.roll(y, 16, 1))
        return y * cosv + partner * sinv

    for j in range(ATTN_HEADS // 2):
        r = norm_rope(a_ref[:, j * LANES:(j + 1) * LANES].astype(F32), qg_ref[...]) * (HEAD_DIM ** -0.5)
        swapped = pltpu.roll(r, HEAD_DIM, 1)
        if j // 2 == 0:
            qa = jnp.where(low, r, 0.0)
            qb = jnp.where(low, swapped, 0.0)
        else:
            qa = jnp.where(low, 0.0, swapped)
            qb = jnp.where(low, 0.0, r)
        q_ref[:, (2 * j) * LANES:(2 * j + 1) * LANES] = qa.astype(BF16)
        q_ref[:, (2 * j + 1) * LANES:(2 * j + 2) * LANES] = qb.astype(BF16)
    rk = norm_rope(a_ref[:, ATTN_WIDTH:ATTN_WIDTH + ATTN_KV_WIDTH].astype(F32), kg_ref[...])
    kt_ref[...] = rk.T.astype(BF16)
    v_ref[...] = a_ref[:, ATTN_WIDTH + ATTN_KV_WIDTH:]


def _attn_prep_call(aqkv, cos_t, sin_t, qg, kg, seg):
    return pl.pallas_call(
        _attn_prep_kernel,
        grid=(BATCH, N_TILES),
        in_specs=[
            pl.BlockSpec((None, TILE, 768), lambda b, t: (b, t, 0)),
            pl.BlockSpec((TILE, LANES), lambda b, t: (t, 0)),
            pl.BlockSpec((TILE, LANES), lambda b, t: (t, 0)),
            _const_spec((1, LANES)),
            _const_spec((1, LANES)),
            _const_spec((LANES, LANES)),
        ],
        out_specs=[
            pl.BlockSpec((None, TILE, ATTN_HEADS * LANES), lambda b, t: (b, t, 0)),
            pl.BlockSpec((None, LANES, TILE), lambda b, t: (b, 0, t)),
            pl.BlockSpec((None, TILE, LANES), lambda b, t: (b, t, 0)),
        ],
        out_shape=[
            jax.ShapeDtypeStruct((BATCH, STREAM, ATTN_HEADS * LANES), BF16),
            jax.ShapeDtypeStruct((BATCH, LANES, STREAM), BF16),
            jax.ShapeDtypeStruct((BATCH, STREAM, LANES), BF16),
        ],
        compiler_params=_params("arbitrary", "arbitrary"),
        name="attn_prep",
    )(aqkv, cos_t, sin_t, qg, kg, seg)


def _attn_kernel(q_ref, kt_ref, v_ref, o_ref, *, with_ctx):
    low = lax.broadcasted_iota(jnp.int32, (TILE, LANES), 1) < HEAD_DIM

    def run(k0, k1):
        kt = kt_ref[:, k0:k1]
        v = v_ref[k0:k1, :]
        for j in range(ATTN_HEADS // 2):
            outs = []
            for e in range(2):
                hq = 2 * j + e
                s = _dot(q_ref[:, hq * LANES:(hq + 1) * LANES], kt)
                p = jnp.exp(s - jnp.max(s, axis=-1, keepdims=True))
                den = jnp.sum(p, axis=-1, keepdims=True)
                outs.append(_dot(p.astype(BF16), v) / den)
            if j // 2 == 0:
                blk = jnp.where(low, outs[0], pltpu.roll(outs[1], HEAD_DIM, 1))
            else:
                blk = jnp.where(low, pltpu.roll(outs[0], HEAD_DIM, 1), outs[1])
            o_ref[:, j * LANES:(j + 1) * LANES] = blk.astype(BF16)

    if with_ctx:
        t = pl.program_id(1)

        @pl.when(t < CTX_TILE)
        def _():
            run(0, STREAM)

        @pl.when(t == CTX_TILE)
        def _():
            run(SEQ, STREAM)
    else:
        run(0, STREAM)


def _attn_call(q_pad, kt, v, with_ctx):
    nt = N_TILES if with_ctx else N_LAT_TILES
    return pl.pallas_call(
        functools.partial(_attn_kernel, with_ctx=with_ctx),
        grid=(BATCH, nt),
        in_specs=[
            pl.BlockSpec((None, TILE, ATTN_HEADS * LANES), lambda b, t: (b, t, 0)),
            pl.BlockSpec((None, LANES, STREAM), lambda b, t: (b, 0, 0)),
            pl.BlockSpec((None, STREAM, LANES), lambda b, t: (b, 0, 0)),
        ],
        out_specs=pl.BlockSpec((None, TILE, ATTN_WIDTH), lambda b, t: (b, t, 0)),
        out_shape=jax.ShapeDtypeStruct((BATCH, STREAM, ATTN_WIDTH), BF16),
        compiler_params=_params("arbitrary", "arbitrary"),
        name="attention",
    )(q_pad, kt, v)


def _gdn_kernel(gq_ref, gg_ref, bd_ref, cw_ref, alog_ref, dtb_ref, ng_ref, seg_ref, tril_ref, triu_ref, o_ref,
                q_s, k_s, v_s, g_s, b_s, u_s, w_s, qk_s, qd_s, kd_s, ge_s, st_s, of_s, ob_s):
    C = GDN_CHUNK
    seg = seg_ref[...]

    def prep_tile(i, carry):
        r0 = pl.multiple_of(i * TILE, TILE)
        x = gq_ref[pl.ds(r0, TILE), :].astype(F32)
        p0 = pl.multiple_of(jnp.maximum(r0 - 16, 0), 16)
        n0 = pl.multiple_of(jnp.minimum(r0 + TILE, STREAM - 16), 16)
        first_of_segment = jnp.logical_or(i == 0, i == CTX_TILE)
        last_of_segment = jnp.logical_or(i == N_LAT_TILES - 1, i == CTX_TILE)
        xp = gq_ref[pl.ds(p0, 16), :].astype(F32)[15:16, :] * jnp.where(first_of_segment, 0.0, 1.0)
        xn = gq_ref[pl.ds(n0, 16), :].astype(F32)[0:1, :] * jnp.where(last_of_segment, 0.0, 1.0)
        row = lax.broadcasted_iota(jnp.int32, (TILE, GDN_QKV_WIDTH), 0)
        xdn = jnp.where(row == 0, xp, pltpu.roll(x, 1, 0))
        xup = jnp.where(row == TILE - 1, xn, pltpu.roll(x, TILE - 1, 0))
        c = _silu(cw_ref[0:1, :] * xdn + cw_ref[1:2, :] * x + cw_ref[2:3, :] * xup)
        q = c[:, 0:256]
        k = c[:, 256:512]
        q = q * lax.rsqrt(_seg_sum(q * q, seg) + NORM_EPS)
        k = k * lax.rsqrt(_seg_sum(k * k, seg) + NORM_EPS)
        q_s[pl.ds(r0, TILE), :] = q * (GDN_DK ** -0.5)
        k_s[pl.ds(r0, TILE), :] = k
        v_s[pl.ds(r0, TILE), :] = c[:, 512:768]
        bd = bd_ref[pl.ds(r0, TILE), :]
        b_s[pl.ds(r0, TILE), :] = _sigmoid(bd)
        g_s[pl.ds(r0, TILE), :] = -jnp.exp(alog_ref[...]) * _softplus(bd + dtb_ref[...])
        return carry

    lax.fori_loop(0, N_TILES, prep_tile, 0)

    ri = lax.broadcasted_iota(jnp.int32, (C, C), 0)
    ci = lax.broadcasted_iota(jnp.int32, (C, C), 1)
    lane = lax.broadcasted_iota(jnp.int32, (C, LANES), 1)
    eye = jnp.where(ri == ci, 1.0, 0.0)
    same16 = (ri // 16) == (ci // 16)
    off32 = jnp.logical_and((ri // 32) == (ci // 32), jnp.logical_not(same16))
    off64 = (ri // 32) != (ci // 32)

    def phase1(c, carry):
        r0 = pl.multiple_of(c * C, C)
        kc = k_s[pl.ds(r0, C), :]
        qc = q_s[pl.ds(r0, C), :]
        vc = v_s[pl.ds(r0, C), :]
        gt = g_s[pl.ds(r0, C), :]
        bt = b_s[pl.ds(r0, C), :]
        g_fwd = _tri_dot(tril_ref[...], gt)
        g_bwd = _tri_dot(triu_ref[...], gt)
        gc = jnp.where(lane < 8 + GDN_HEADS, g_fwd, g_bwd)
        g_tot = g_fwd[C - 1:C, :]
        gct = jnp.concatenate([gc, jnp.zeros((LANES - C, LANES), F32)], axis=0).T
        ge_s[c] = jnp.broadcast_to(jnp.exp(g_tot), (8, LANES))
        e_g = jnp.exp(gc)
        e_gk = jnp.exp(g_tot - gc)
        for h in range(GDN_HEADS):
            hs = slice(h * GDN_DK, (h + 1) * GDN_DK)
            kh = kc[:, hs]
            qh = qc[:, hs]
            vh = vc[:, hs]
            khb = kh.astype(BF16)
            kk = _dot_nt(khb, khb)
            qk0 = _dot_nt(qh.astype(BF16), khb)
            for d in range(2):
                idx = d * GDN_HEADS + h
                gcol = gc[:, 8 + idx:9 + idx]
                grow = gct[8 + idx:9 + idx, 0:C]
                incl = (ri >= ci) if d == 0 else (ri <= ci)
                strict = (ri > ci) if d == 0 else (ri < ci)
                decay = jnp.where(incl, jnp.exp(gcol - grow), 0.0)
                bcol = bt[:, idx:idx + 1]
                egc = e_g[:, 8 + idx:9 + idx]
                a = jnp.where(strict, bcol * kk * decay, 0.0)
                x = jnp.where(same16, -a, 0.0)
                xb = x.astype(BF16)
                inv = eye + x
                y = _dot(xb, xb)
                for stage in range(3):
                    yb = y.astype(BF16)
                    if stage < 2:
                        prod = _dot(yb, jnp.concatenate([inv, y], axis=1).astype(BF16))
                        inv = inv + prod[:, 0:C]
                        y = prod[:, C:2 * C]
                    else:
                        inv = inv + _dot(yb, inv.astype(BF16))
                for off in (jnp.where(off32, a, 0.0), jnp.where(off64, a, 0.0)):
                    invb = inv.astype(BF16)
                    inv = inv - _dot(_dot(invb, off.astype(BF16)).astype(BF16), invb)
                r = _dot(inv.astype(BF16),
                         jnp.concatenate([vh * bcol, kh * (bcol * egc)], axis=1).astype(BF16))
                u_s[c, d, :, hs] = r[:, 0:GDN_DK]
                w_s[c, d, :, hs] = r[:, GDN_DK:2 * GDN_DK].astype(BF16)
                qk_s[c, d, :, hs] = (qk0 * decay).astype(BF16)
                qd_s[c, d, :, hs] = (qh * egc).astype(BF16)
                kd_s[c, d, :, hs] = (kh * e_gk[:, 8 + idx:9 + idx]).astype(BF16)
        return carry

    lax.fori_loop(0, N_CHUNKS, phase1, 0)

    st_s[...] = jnp.zeros(st_s.shape, F32)

    def phase2(s, carry):
        for d in range(2):
            if d == 0:
                c = jnp.where(s < N_CTX_CHUNKS, s + N_LAT_CHUNKS, s - N_CTX_CHUNKS)
            else:
                c = N_CHUNKS - 1 - s
            r0 = pl.multiple_of(c * C, C)
            ge_row = ge_s[c]
            o_dst = of_s if d == 0 else ob_s
            for h in range(GDN_HEADS):
                idx = d * GDN_HEADS + h
                hs = slice(h * GDN_DK, (h + 1) * GDN_DK)
                st = st_s[idx]
                stb = st.astype(BF16)
                vn = u_s[c, d, :, hs] - _dot(w_s[c, d, :, hs], stb)
                vnb = vn.astype(BF16)
                o_dst[pl.ds(r0, C), hs] = _dot(qd_s[c, d, :, hs], stb) + _dot(qk_s[c, d, :, hs], vnb)
                st_s[idx] = st * ge_row[0:1, 8 + idx:9 + idx] + _dot_tn(kd_s[c, d, :, hs], vnb)
        return carry

    lax.fori_loop(0, N_CHUNKS, phase2, 0)

    def finish_tile(i, carry):
        r0 = pl.multiple_of(i * TILE, TILE)
        o = of_s[pl.ds(r0, TILE), :] + ob_s[pl.ds(r0, TILE), :]
        ms = _seg_sum(o * o, seg) * (1.0 / GDN_DK)
        y = o * lax.rsqrt(ms + NORM_EPS) * ng_ref[...]
        o_ref[pl.ds(r0, TILE), :] = (y * _silu(gg_ref[pl.ds(r0, TILE), :].astype(F32))).astype(BF16)
        return carry

    lax.fori_loop(0, N_TILES, finish_tile, 0)


def _gdn_call(gqkv, ggate, bd, conv_w, alog_row, dtb_row, ng_row, seg, tril, triu):
    def per_batch(width):
        return pl.BlockSpec((None, STREAM, width), lambda b: (b, 0, 0))
    packed = (N_CHUNKS, 2, GDN_CHUNK, GDN_WIDTH)
    return pl.pallas_call(
        _gdn_kernel,
        grid=(BATCH,),
        in_specs=[
            per_batch(768), per_batch(256), per_batch(LANES),
            _const_spec((3, 768)), _const_spec((1, LANES)), _const_spec((1, LANES)), _const_spec((1, GDN_WIDTH)),
            _const_spec((GDN_WIDTH, GDN_WIDTH)), _const_spec((GDN_CHUNK, GDN_CHUNK)),
            _const_spec((GDN_CHUNK, GDN_CHUNK)),
        ],
        out_specs=per_batch(GDN_WIDTH),
        out_shape=jax.ShapeDtypeStruct((BATCH, STREAM, GDN_WIDTH), BF16),
        scratch_shapes=[
            pltpu.VMEM((STREAM, GDN_WIDTH), F32),
            pltpu.VMEM((STREAM, GDN_WIDTH), F32),
            pltpu.VMEM((STREAM, GDN_WIDTH), F32),
            pltpu.VMEM((STREAM, LANES), F32),
            pltpu.VMEM((STREAM, LANES), F32),
            pltpu.VMEM(packed, F32),
            pltpu.VMEM(packed, BF16),
            pltpu.VMEM(packed, BF16),
            pltpu.VMEM(packed, BF16),
            pltpu.VMEM(packed, BF16),
            pltpu.VMEM((N_CHUNKS, 8, LANES), F32),
            pltpu.VMEM((2 * GDN_HEADS, GDN_DK, GDN_DK), F32),
            pltpu.VMEM((STREAM, GDN_WIDTH), F32),
            pltpu.VMEM((STREAM, GDN_WIDTH), F32),
        ],
        compiler_params=_params("arbitrary"),
        name="gdn",
    )(gqkv, ggate, bd, conv_w, alog_row, dtb_row, ng_row, seg, tril, triu)


def _hyena_filter_kernel(feat_ref, w1_ref, b1_ref, w2_ref, b2_ref, w3_ref, fr_ref, win_ref, fc_ref,
                         p_ref, q_ref, r_ref, *, n):
    fr = fr_ref[...]
    hid = jnp.sin(fr * (_dot(feat_ref[...].astype(BF16), w1_ref[...]) + b1_ref[...]))
    hid = jnp.sin(fr * (_dot(hid.astype(BF16), w2_ref[...]) + b2_ref[...]))
    filt = _dot(hid.astype(BF16), w3_ref[...])
    win = win_ref[...]
    row = lax.broadcasted_iota(jnp.int32, (n, HYENA_WIDTH), 0)
    h_fwd = filt[:, 0:HYENA_WIDTH] * win
    h_bwd = jnp.where(row == 0, 0.0, filt[:, HYENA_WIDTH:] * win)
    hcat = jnp.concatenate([h_fwd, h_bwd], axis=1).astype(BF16)
    h_re = _dot(fc_ref[:, 0:n], hcat)
    h_im = _dot(fc_ref[:, n:2 * n], hcat)
    sgn = jnp.where((row & 1) == 0, 1.0, -1.0)
    nyq = jnp.sum(sgn * (h_fwd + h_bwd), axis=0, keepdims=True)
    re = h_re[:, 0:HYENA_WIDTH] + h_re[:, HYENA_WIDTH:]
    p_ref[...] = re
    q_ref[...] = h_im[:, 0:HYENA_WIDTH] - h_im[:, HYENA_WIDTH:]
    r_ref[...] = jnp.where(row == 0, nyq, re)


def _hyena_filter_call(n, feats, w1, b1, w2, b2, w3, fr, win, fcat):
    shapes = [(n, LANES), (LANES, HYENA_FILTER_HIDDEN), (1, HYENA_FILTER_HIDDEN),
              (HYENA_FILTER_HIDDEN, HYENA_FILTER_HIDDEN), (1, HYENA_FILTER_HIDDEN),
              (HYENA_FILTER_HIDDEN, 2 * HYENA_WIDTH), (1, HYENA_FILTER_HIDDEN), (n, HYENA_WIDTH), (n, 2 * n)]
    out = jax.ShapeDtypeStruct((n, HYENA_WIDTH), F32)
    return pl.pallas_call(
        functools.partial(_hyena_filter_kernel, n=n),
        grid=(1,),
        in_specs=[_const_spec(s) for s in shapes],
        out_specs=[pl.BlockSpec((n, HYENA_WIDTH), lambda i: (0, 0))] * 3,
        out_shape=[out, out, out],
        compiler_params=_params("arbitrary"),
        name=f"hyena_filter_{n}",
    )(feats, w1, b1, w2, b2, w3, fr, win, fcat)


def _hyena_kernel(x_ref, cw_ref, bias_ref, fc_ref, p_ref, q_ref, r_ref, o_ref, w_s, x0_s, z_s, *, n):
    rows = min(n, TILE)
    n_tiles = n // rows
    inv_n = 1.0 / (2 * n)

    def conv_tile(i, carry):
        r0 = pl.multiple_of(i * rows, rows)
        x = x_ref[pl.ds(r0, rows), :].astype(F32)
        p0 = pl.multiple_of(jnp.maximum(r0 - 16, 0), 16)
        n0 = pl.multiple_of(jnp.minimum(r0 + rows, n - 16), 16)
        xp = x_ref[pl.ds(p0, 16), :].astype(F32)[15:16, :] * jnp.where(i == 0, 0.0, 1.0)
        xn = x_ref[pl.ds(n0, 16), :].astype(F32)[0:1, :] * jnp.where(i == n_tiles - 1, 0.0, 1.0)
        row = lax.broadcasted_iota(jnp.int32, (rows, 3 * HYENA_WIDTH), 0)
        xdn = jnp.where(row == 0, xp, pltpu.roll(x, 1, 0))
        xup = jnp.where(row == rows - 1, xn, pltpu.roll(x, rows - 1, 0))
        c = cw_ref[0:1, :] * xdn + cw_ref[1:2, :] * x + cw_ref[2:3, :] * xup
        x0_s[pl.ds(r0, rows), :] = c[:, 0:HYENA_WIDTH]
        w_s[pl.ds(r0, rows), :] = c[:, 2 * HYENA_WIDTH:] * c[:, HYENA_WIDTH:2 * HYENA_WIDTH]
        return carry

    lax.fori_loop(0, n_tiles, conv_tile, 0)

    w = w_s[...]
    wb = w.astype(BF16)
    u_re = _dot(fc_ref[:, 0:n], wb)
    u_im = _dot(fc_ref[:, n:2 * n], wb)
    row = lax.broadcasted_iota(jnp.int32, (n, HYENA_WIDTH), 0)
    sgn = jnp.where((row & 1) == 0, 1.0, -1.0)
    u_nyq = jnp.sum(sgn * w, axis=0, keepdims=True)
    u_im = jnp.where(row == 0, u_nyq, u_im)
    fscale = jnp.where(row == 0, inv_n, 2.0 * inv_n)
    z_re = u_re * p_ref[...] - u_im * q_ref[...]
    z_im = u_re * q_ref[...] + u_im * r_ref[...]
    z_s[0:n, :] = (z_re * fscale).astype(BF16)
    z_s[n:2 * n, :] = (z_im * fscale).astype(BF16)
    y = _dot(fc_ref[...], z_s[...]) + sgn * (z_im[0:1, :] * inv_n) + w * bias_ref[...]
    o_ref[...] = (y * x0_s[...]).astype(BF16)


def _hyena_call(n, tile_index, hy, conv_w, bias, fcat, hp, hq, hr):
    return pl.pallas_call(
        functools.partial(_hyena_kernel, n=n),
        grid=(BATCH,),
        in_specs=[
            pl.BlockSpec((None, n, 3 * HYENA_WIDTH), lambda b: (b, tile_index, 0)),
            _const_spec((3, 3 * HYENA_WIDTH)),
            _const_spec((1, HYENA_WIDTH)),
            _const_spec((n, 2 * n)),
            _const_spec((n, HYENA_WIDTH)), _const_spec((n, HYENA_WIDTH)), _const_spec((n, HYENA_WIDTH)),
        ],
        out_specs=pl.BlockSpec((None, n, HYENA_WIDTH), lambda b: (b, 0, 0)),
        out_shape=jax.ShapeDtypeStruct((BATCH, n, HYENA_WIDTH), BF16),
        scratch_shapes=[
            pltpu.VMEM((n, HYENA_WIDTH), F32),
            pltpu.VMEM((n, HYENA_WIDTH), F32),
            pltpu.VMEM((2 * n, HYENA_WIDTH), BF16),
        ],
        compiler_params=_params("arbitrary"),
        name=f"hyena_{n}",
    )(hy, conv_w, bias, fcat, hp, hq, hr)


def _out_kernel(h_ref, a_ref, g_ref, yl_ref, yc_ref, m_ref, w_ref, o_ref, *, with_ctx):
    if with_ctx:
        y = jnp.where(pl.program_id(1) == CTX_TILE, yc_ref[...], yl_ref[...])
    else:
        y = yl_ref[...]
    acc = _dot(a_ref[...], w_ref[0:512, :])
    acc = acc + _dot(g_ref[...], w_ref[512:768, :])
    acc = acc + _dot(y, w_ref[768:1024, :])
    o_ref[...] = h_ref[...] + m_ref[2:3, :] * acc


def _out_call(h, attn, gdn, hy_lat, hy_ctx, mod, w_out, with_ctx):
    nt = N_TILES if with_ctx else N_LAT_TILES

    def tile(width):
        return pl.BlockSpec((None, TILE, width), lambda b, t: (b, t, 0))
    return pl.pallas_call(
        functools.partial(_out_kernel, with_ctx=with_ctx),
        grid=(BATCH, nt),
        in_specs=[
            tile(D_MODEL), tile(ATTN_WIDTH), tile(GDN_WIDTH),
            pl.BlockSpec((None, TILE, HYENA_WIDTH), lambda b, t: (b, jnp.minimum(t, N_LAT_TILES - 1), 0)),
            pl.BlockSpec((None, TILE, HYENA_WIDTH), lambda b, t: (b, 0, 0)),
            pl.BlockSpec((None, 6, D_MODEL), lambda b, t: (_mod_row(b, t), 0, 0)),
            _const_spec((D_MODEL, D_MODEL)),
        ],
        out_specs=tile(D_MODEL),
        out_shape=jax.ShapeDtypeStruct((BATCH, STREAM, D_MODEL), F32),
        input_output_aliases={0: 0},
        compiler_params=_params("arbitrary", "arbitrary"),
        name="out_proj",
    )(h, attn, gdn, hy_lat, hy_ctx, mod, w_out)


def _ffn_kernel(h_ref, hp_ref, hn_ref, m_ref, g_ref, wu_ref, cw_ref, wd_ref, o_ref, acc_s):
    t = pl.program_id(1)
    ext = TILE + 2 * HALO
    x = jnp.concatenate([hp_ref[...], h_ref[...], hn_ref[...]], axis=0)
    u = _norm_mod(x, g_ref[...], m_ref[3:4, :], m_ref[4:5, :]).astype(BF16)
    first_of_segment = jnp.logical_or(t == 0, t == CTX_TILE)
    last_of_segment = jnp.logical_or(t == N_LAT_TILES - 1, t == CTX_TILE)
    row = lax.broadcasted_iota(jnp.int32, (ext, 2 * FF_CHUNK), 0)
    keep = jnp.where(row < HALO, jnp.where(first_of_segment, 0.0, 1.0),
                     jnp.where(row >= HALO + TILE, jnp.where(last_of_segment, 0.0, 1.0), 1.0))
    acc_s[...] = jnp.zeros(acc_s.shape, F32)

    def chunk(j, carry):
        p = _dot(u, wu_ref[j]) * keep
        cw = cw_ref[j]
        c = (cw[0:1, :] * pltpu.roll(p, 1, 0) + cw[1:2, :] * p + cw[2:3, :] * pltpu.roll(p, ext - 1, 0))
        c = c[HALO:HALO + TILE, :]
        act = (_silu(c[:, 0:FF_CHUNK]) * c[:, FF_CHUNK:]).astype(BF16)
        acc_s[...] += _dot(act, wd_ref[j])
        return carry

    lax.fori_loop(0, N_FF_CHUNKS, chunk, 0)
    o_ref[...] = h_ref[...] + m_ref[5:6, :] * acc_s[...]


def _ffn_call(h, mod, gain, w_up, conv_w, w_down, with_ctx):
    nt = N_TILES if with_ctx else N_LAT_TILES
    per = TILE // HALO
    last = STREAM // HALO - 1
    return pl.pallas_call(
        _ffn_kernel,
        grid=(BATCH, nt),
        in_specs=[
            pl.BlockSpec((None, TILE, D_MODEL), lambda b, t: (b, t, 0)),
            pl.BlockSpec((None, HALO, D_MODEL), lambda b, t: (b, jnp.maximum(t * per - 1, 0), 0)),
            pl.BlockSpec((None, HALO, D_MODEL), lambda b, t: (b, jnp.minimum((t + 1) * per, last), 0)),
            pl.BlockSpec((None, 6, D_MODEL), lambda b, t: (_mod_row(b, t), 0, 0)),
            _const_spec((1, D_MODEL)),
            _const_spec((N_FF_CHUNKS, D_MODEL, 2 * FF_CHUNK)),
            _const_spec((N_FF_CHUNKS, 8, 2 * FF_CHUNK)),
            _const_spec((N_FF_CHUNKS, FF_CHUNK, D_MODEL)),
        ],
        out_specs=pl.BlockSpec((None, TILE, D_MODEL), lambda b, t: (b, t, 0)),
        out_shape=jax.ShapeDtypeStruct((BATCH, STREAM, D_MODEL), F32),
        scratch_shapes=[pltpu.VMEM((TILE, D_MODEL), F32)],
        compiler_params=_params("arbitrary", "arbitrary"),
        name="conv_ffn",
    )(h, h, h, mod, gain, w_up, conv_w, w_down)


def _final_kernel(h_ref, g_ref, o_ref):
    x = h_ref[...]
    ms = jnp.mean(x * x, axis=-1, keepdims=True)
    o_ref[...] = x * lax.rsqrt(ms + NORM_EPS) * g_ref[...]


def _final_call(h, gain):
    return pl.pallas_call(
        _final_kernel,
        grid=(BATCH, N_LAT_TILES),
        in_specs=[pl.BlockSpec((None, TILE, D_MODEL), lambda b, t: (b, t, 0)), _const_spec((1, D_MODEL))],
        out_specs=pl.BlockSpec((None, TILE, D_MODEL), lambda b, t: (b, t, 0)),
        out_shape=jax.ShapeDtypeStruct((BATCH, SEQ, D_MODEL), F32),
        compiler_params=_params("arbitrary", "arbitrary"),
        name="final_norm",
    )(h, gain)


def _block_diag_ones(width, block):
    i = jnp.arange(width)
    return (i[:, None] // block == i[None, :] // block).astype(BF16)


def _rope_tables():
    n = jnp.arange(SEQ)
    lane = jnp.arange(LANES)
    d = lane % HEAD_DIM
    comp = jnp.where((d < HEAD_DIM // 2)[None, :], (n // GRID_W)[:, None], (n % GRID_W)[:, None]).astype(F32)
    inv = ROPE_THETA ** (-(d % 16).astype(F32) / 16.0)
    ang = comp * inv[None, :]
    sign = jnp.where((d % 32) < 16, -1.0, 1.0)[None, :]
    cos_t = jnp.concatenate([jnp.cos(ang), jnp.ones((CTX_LEN, LANES), F32)], axis=0)
    sin_t = jnp.concatenate([jnp.sin(ang) * sign, jnp.zeros((CTX_LEN, LANES), F32)], axis=0)
    return cos_t, sin_t


def _dft_table(n):
    f = jnp.arange(n, dtype=jnp.int32)
    m = (f[:, None] * f[None, :]) % (2 * n)
    ang = m.astype(F32) * (math.pi / n)
    return jnp.concatenate([jnp.cos(ang), -jnp.sin(ang)], axis=1).astype(BF16)


def _hyena_static(n):
    t = jnp.linspace(0.0, 1.0, n, dtype=F32)[:, None]
    omega = 2.0 * math.pi * jnp.arange(n, dtype=F32)[:, None] / n
    bands = jnp.linspace(1e-4, HYENA_EMB_BANDS - 1, HYENA_EMB_BANDS, dtype=F32)[None, :]
    feats = jnp.concatenate([t, jnp.cos(bands * omega), -jnp.sin(bands * omega)], axis=-1)
    feats = jnp.pad(feats, ((0, 0), (0, LANES - feats.shape[1])))
    max_decay = math.log(HYENA_DECAY_TARGET) / HYENA_FAST_DECAY
    min_decay = math.log(HYENA_DECAY_TARGET) / HYENA_SLOW_DECAY
    deltas = jnp.abs(jnp.linspace(min_decay, max_decay, HYENA_WIDTH, dtype=F32))
    window = jnp.exp(-t * deltas[None, :])
    return feats, window


def kernel(x, c, ctx, c_ctx, w_ada, b_ada, norm1, norm2, w_in, w_out, q_gain, k_gain, gdn_conv, gdn_a_log,
           gdn_dt_bias, gdn_norm, hyena_conv, hyena_w1, hyena_b1, hyena_w2, hyena_b2, hyena_w3, hyena_freq,
           hyena_bias, ffn_up, ffn_conv, ffn_down, final_norm):
    assert x.shape == (BATCH, SEQ, D_MODEL) and ctx.shape == (BATCH, CTX_LEN, D_MODEL)

    h = jnp.concatenate([x, ctx], axis=1)
    cvec = jnp.concatenate([c, c_ctx[None, :], jnp.zeros((MOD_ROWS - BATCH - 1, D_MODEL), F32)], axis=0)
    mod_all = _ada_call(cvec, w_ada, b_ada).reshape(DEPTH, MOD_ROWS, 6, D_MODEL)

    cos_t, sin_t = _rope_tables()
    seg_head = _block_diag_ones(LANES, HEAD_DIM)
    seg_gdn = _block_diag_ones(GDN_WIDTH, GDN_DK)
    ii = jnp.arange(GDN_CHUNK)
    tril = (ii[:, None] >= ii[None, :]).astype(BF16)
    triu = (ii[:, None] <= ii[None, :]).astype(BF16)
    fcat_lat = _dft_table(SEQ)
    fcat_ctx = _dft_table(CTX_LEN)
    feats_lat, win_lat = _hyena_static(SEQ)
    feats_ctx, win_ctx = _hyena_static(CTX_LEN)

    for i in range(DEPTH):
        with_ctx = i < DEPTH - 1
        mod = mod_all[i]
        wi = w_in[i]
        w_cat = jnp.concatenate(
            [wi[:, 0:768], wi[:, 768:1536], wi[:, 1536:1792], wi[:, 1808:2576], wi[:, 1792:1808],
             jnp.zeros((D_MODEL, LANES - 16), F32)], axis=1).astype(BF16)
        aqkv, gqkv, ggate, hyp, bd = _in_call(h, mod, norm1[i][None, :], w_cat)

        q_pad, kt, v = _attn_prep_call(aqkv, cos_t, sin_t, jnp.tile(q_gain[i], 2)[None, :],
                                       jnp.tile(k_gain[i], 2)[None, :], seg_head)
        attn = _attn_call(q_pad, kt, v, with_ctx)

        pad8 = jnp.zeros((8,), F32)
        alog_row = jnp.concatenate([pad8, gdn_a_log[i].reshape(-1), jnp.zeros((LANES - 16,), F32)])[None, :]
        dtb_row = jnp.concatenate([pad8, gdn_dt_bias[i].reshape(-1), jnp.zeros((LANES - 16,), F32)])[None, :]
        gdn = _gdn_call(gqkv, ggate, bd, gdn_conv[i], alog_row, dtb_row, jnp.tile(gdn_norm[i], GDN_HEADS)[None, :],
                        seg_gdn, tril, triu)

        w1 = jnp.pad(hyena_w1[i], ((0, LANES - hyena_w1.shape[1]), (0, 0))).astype(BF16)
        filt_args = (w1, hyena_b1[i][None, :], hyena_w2[i].astype(BF16), hyena_b2[i][None, :],
                     hyena_w3[i].astype(BF16), hyena_freq[i][None, :])
        hp, hq, hr = _hyena_filter_call(SEQ, feats_lat, *filt_args, win_lat, fcat_lat)
        hy_lat = _hyena_call(SEQ, 0, hyp, hyena_conv[i], hyena_bias[i], fcat_lat, hp, hq, hr)
        if with_ctx:
            hp, hq, hr = _hyena_filter_call(CTX_LEN, feats_ctx, *filt_args, win_ctx, fcat_ctx)
            hy_ctx = _hyena_call(CTX_LEN, CTX_TILE, hyp, hyena_conv[i], hyena_bias[i], fcat_ctx, hp, hq, hr)
        else:
            hy_ctx = hy_lat

        h = _out_call(h, attn, gdn, hy_lat, hy_ctx, mod, w_out[i].astype(BF16), with_ctx)

        wu = ffn_up[i]
        w_up = jnp.concatenate([wu[:, 0:D_FF].reshape(D_MODEL, N_FF_CHUNKS, FF_CHUNK),
                                wu[:, D_FF:].reshape(D_MODEL, N_FF_CHUNKS, FF_CHUNK)], axis=2)
        w_up = jnp.transpose(w_up, (1, 0, 2)).astype(BF16)
        fc = ffn_conv[i]
        conv_w = jnp.concatenate([fc[:, 0:D_FF].reshape(3, N_FF_CHUNKS, FF_CHUNK),
                                  fc[:, D_FF:].reshape(3, N_FF_CHUNKS, FF_CHUNK)], axis=2)
        conv_w = jnp.pad(jnp.transpose(conv_w, (1, 0, 2)), ((0, 0), (0, 5), (0, 0)))
        w_down = ffn_down[i].astype(BF16).reshape(N_FF_CHUNKS, FF_CHUNK, D_MODEL)
        h = _ffn_call(h, mod, norm2[i][None, :], w_up, conv_w, w_down, with_ctx)

    return _final_call(h, final_norm[None, :])
```

```python
import functools
import math

import jax
import jax.numpy as jnp
from jax import lax
from jax.experimental import pallas as pl
from jax.experimental.pallas import tpu as pltpu

F32 = jnp.float32
BF16 = jnp.bfloat16

D_MODEL = 1024
BATCH = 16
SEQ = 2048
DEPTH = 4
CTX_LEN = 256
STREAM = SEQ + CTX_LEN
TILE = 256
N_TILES = STREAM // TILE
N_LAT_TILES = SEQ // TILE
CTX_TILE = N_LAT_TILES
GRID_W = 64
NORM_EPS = 1e-6
HEAD_DIM = 64
LANES = 128
MOD_ROWS = 24
CTX_MOD_ROW = BATCH

ATTN_WIDTH = 512
ATTN_HEADS = 8
ATTN_KV_WIDTH = 128
ROPE_THETA = 10000.0

GDN_WIDTH = 256
GDN_HEADS = 4
GDN_DK = 64
GDN_QKV_WIDTH = 768
GDN_CHUNK = 64
N_CHUNKS = STREAM // GDN_CHUNK
N_LAT_CHUNKS = SEQ // GDN_CHUNK
N_CTX_CHUNKS = CTX_LEN // GDN_CHUNK
GDN_GROUP = 4

HYENA_WIDTH = 256
HYENA_EMB_BANDS = 16
HYENA_FILTER_HIDDEN = 64
HYENA_FAST_DECAY = 0.3
HYENA_SLOW_DECAY = 1.5
HYENA_DECAY_TARGET = 1e-2

D_FF = 2816
FF_CHUNK = 256
N_FF_CHUNKS = D_FF // FF_CHUNK
HALO = 8

COL_ATTN = (0, 768)
COL_GQKV = (768, 1536)
COL_GGATE = (1536, 1792)
COL_HY = (1792, 2560)
COL_BD = (2560, 2688)
IN_COLS = 2688

VMEM_LIMIT = 56 * 1024 * 1024


def _dot(a, b):
    return jnp.dot(a, b, preferred_element_type=F32)


def _dot_nt(a, b):
    return lax.dot_general(a, b, (((1,), (1,)), ((), ())), preferred_element_type=F32)


def _dot_tn(a, b):
    return lax.dot_general(a, b, (((0,), (0,)), ((), ())), preferred_element_type=F32)


def _sigmoid(x):
    return 1.0 / (1.0 + jnp.exp(-x))


def _silu(x):
    return x * _sigmoid(x)


def _softplus(x):
    return jnp.maximum(x, 0.0) + jnp.log(1.0 + jnp.exp(-jnp.abs(x)))


def _seg_sum(x, seg):
    hi = x.astype(BF16)
    lo = (x - hi.astype(F32)).astype(BF16)
    return _dot(hi, seg) + _dot(lo, seg)


def _tri_dot(tri, x):
    x1 = x.astype(BF16)
    r1 = x - x1.astype(F32)
    x2 = r1.astype(BF16)
    x3 = (r1 - x2.astype(F32)).astype(BF16)
    return _dot(tri, x1) + _dot(tri, x2) + _dot(tri, x3)


def _dot_split3(x, sel):
    x1 = x.astype(BF16)
    r1 = x - x1.astype(F32)
    x2 = r1.astype(BF16)
    x3 = (r1 - x2.astype(F32)).astype(BF16)
    return _dot(x1, sel) + _dot(x2, sel) + _dot(x3, sel)


def _params(*sem):
    return pltpu.CompilerParams(dimension_semantics=sem, vmem_limit_bytes=VMEM_LIMIT)


def _const_spec(shape):
    nd = len(shape)
    return pl.BlockSpec(shape, lambda *_: (0,) * nd, pipeline_mode=pl.Buffered(1))


def _mod_row(b, t):
    return jnp.where(t == CTX_TILE, CTX_MOD_ROW, b)


def _ada_kernel(c_ref, w_ref, b_ref, o_ref):
    s = _silu(c_ref[...]).astype(BF16)
    o_ref[...] = _dot(s, w_ref[...].astype(BF16)) + b_ref[...]


def _ada_call(cvec, w_ada, b_ada):
    nj = 6
    return pl.pallas_call(
        _ada_kernel,
        grid=(DEPTH, nj),
        in_specs=[
            pl.BlockSpec((MOD_ROWS, D_MODEL), lambda i, j: (0, 0)),
            pl.BlockSpec((None, D_MODEL, D_MODEL), lambda i, j: (i, 0, j)),
            pl.BlockSpec((None, 1, D_MODEL), lambda i, j: (i, 0, j)),
        ],
        out_specs=pl.BlockSpec((None, MOD_ROWS, D_MODEL), lambda i, j: (i, 0, j)),
        out_shape=jax.ShapeDtypeStruct((DEPTH, MOD_ROWS, 6 * D_MODEL), F32),
        compiler_params=_params("arbitrary", "arbitrary"),
        name="adaln",
    )(cvec, w_ada, b_ada.reshape(DEPTH, 1, 6 * D_MODEL))


def _norm_mod(x, gain, shift, scale):
    ms = jnp.mean(x * x, axis=-1, keepdims=True)
    return (x * lax.rsqrt(ms + NORM_EPS) * gain) * (1.0 + scale) + shift


def _in_kernel(x_ref, m_ref, g_ref, w_ref, oa_ref, ogq_ref, ogg_ref, ohy_ref, obd_ref):
    u = _norm_mod(x_ref[...], g_ref[...], m_ref[0:1, :], m_ref[1:2, :]).astype(BF16)
    oa_ref[...] = _dot(u, w_ref[:, COL_ATTN[0]:COL_ATTN[1]]).astype(BF16)
    ogq_ref[...] = _dot(u, w_ref[:, COL_GQKV[0]:COL_GQKV[1]]).astype(BF16)
    ogg_ref[...] = _dot(u, w_ref[:, COL_GGATE[0]:COL_GGATE[1]]).astype(BF16)
    ohy_ref[...] = _dot(u, w_ref[:, COL_HY[0]:COL_HY[1]]).astype(BF16)
    obd_ref[...] = _dot(u, w_ref[:, COL_BD[0]:COL_BD[1]])


def _in_call(h, mod, gain, w_cat):
    def tile(width):
        return pl.BlockSpec((None, TILE, width), lambda b, t: (b, t, 0))
    return pl.pallas_call(
        _in_kernel,
        grid=(BATCH, N_TILES),
        in_specs=[
            tile(D_MODEL),
            pl.BlockSpec((None, 6, D_MODEL), lambda b, t: (_mod_row(b, t), 0, 0)),
            _const_spec((1, D_MODEL)),
            _const_spec((D_MODEL, IN_COLS)),
        ],
        out_specs=[tile(768), tile(768), tile(256), tile(768), tile(LANES)],
        out_shape=[
            jax.ShapeDtypeStruct((BATCH, STREAM, 768), BF16),
            jax.ShapeDtypeStruct((BATCH, STREAM, 768), BF16),
            jax.ShapeDtypeStruct((BATCH, STREAM, 256), BF16),
            jax.ShapeDtypeStruct((BATCH, STREAM, 768), BF16),
            jax.ShapeDtypeStruct((BATCH, STREAM, LANES), F32),
        ],
        compiler_params=_params("arbitrary", "arbitrary"),
        name="in_proj",
    )(h, mod, gain, w_cat)


def _attn_prep_kernel(a_ref, cos_ref, sin_ref, qg_ref, kg_ref, seg_ref, q_ref, kt_ref, v_ref):
    lane = lax.broadcasted_iota(jnp.int32, (TILE, LANES), 1)
    first = (lane % 32) < 16
    low = lane < HEAD_DIM
    seg = seg_ref[...]
    cosv = cos_ref[...]
    sinv = sin_ref[...]

    def norm_rope(x, gain):
        ms = _seg_sum(x * x, seg) * (1.0 / HEAD_DIM)
        y = x * lax.rsqrt(ms + NORM_EPS) * gain
        partner = jnp.where(first, pltpu.roll(y, LANES - 16, 1), pltpu.roll(y, 16, 1))
        return y * cosv + partner * sinv

    for j in range(ATTN_HEADS // 2):
        r = norm_rope(a_ref[:, j * LANES:(j + 1) * LANES].astype(F32), qg_ref[...]) * (HEAD_DIM ** -0.5)
        swapped = pltpu.roll(r, HEAD_DIM, 1)
        if j // 2 == 0:
            qa = jnp.where(low, r, 0.0)
            qb = jnp.where(low, swapped, 0.0)
        else:
            qa = jnp.where(low, 0.0, swapped)
            qb = jnp.where(low, 0.0, r)
        q_ref[:, (2 * j) * LANES:(2 * j + 1) * LANES] = qa.astype(BF16)
        q_ref[:, (2 * j + 1) * LANES:(2 * j + 2) * LANES] = qb.astype(BF16)
    rk = norm_rope(a_ref[:, ATTN_WIDTH:ATTN_WIDTH + ATTN_KV_WIDTH].astype(F32), kg_ref[...])
    kt_ref[...] = rk.T.astype(BF16)
    v_ref[...] = a_ref[:, ATTN_WIDTH + ATTN_KV_WIDTH:]


def _attn_prep_call(aqkv, cos_t, sin_t, qg, kg, seg):
    return pl.pallas_call(
        _attn_prep_kernel,
        grid=(BATCH, N_TILES),
        in_specs=[
            pl.BlockSpec((None, TILE, 768), lambda b, t: (b, t, 0)),
            pl.BlockSpec((TILE, LANES), lambda b, t: (t, 0)),
            pl.BlockSpec((TILE, LANES), lambda b, t: (t, 0)),
            _const_spec((1, LANES)),
            _const_spec((1, LANES)),
            _const_spec((LANES, LANES)),
        ],
        out_specs=[
            pl.BlockSpec((None, TILE, ATTN_HEADS * LANES), lambda b, t: (b, t, 0)),
            pl.BlockSpec((None, LANES, TILE), lambda b, t: (b, 0, t)),
            pl.BlockSpec((None, TILE, LANES), lambda b, t: (b, t, 0)),
        ],
        out_shape=[
            jax.ShapeDtypeStruct((BATCH, STREAM, ATTN_HEADS * LANES), BF16),
            jax.ShapeDtypeStruct((BATCH, LANES, STREAM), BF16),
            jax.ShapeDtypeStruct((BATCH, STREAM, LANES), BF16),
        ],
        compiler_params=_params("arbitrary", "arbitrary"),
        name="attn_prep",
    )(aqkv, cos_t, sin_t, qg, kg, seg)


def _attn_kernel(q_ref, kt_ref, v_ref, o_ref, *, with_ctx):
    low = lax.broadcasted_iota(jnp.int32, (TILE, LANES), 1) < HEAD_DIM

    def run(k0, k1):
        kt = kt_ref[:, k0:k1]
        v = v_ref[k0:k1, :]
        for j in range(ATTN_HEADS // 2):
            outs = []
            for e in range(2):
                hq = 2 * j + e
                s = _dot(q_ref[:, hq * LANES:(hq + 1) * LANES], kt)
                p = jnp.exp(s - jnp.max(s, axis=-1, keepdims=True))
                den = jnp.sum(p, axis=-1, keepdims=True)
                outs.append(_dot(p.astype(BF16), v) / den)
            if j // 2 == 0:
                blk = jnp.where(low, outs[0], pltpu.roll(outs[1], HEAD_DIM, 1))
            else:
                blk = jnp.where(low, pltpu.roll(outs[0], HEAD_DIM, 1), outs[1])
            o_ref[:, j * LANES:(j + 1) * LANES] = blk.astype(BF16)

    if with_ctx:
        t = pl.program_id(1)

        @pl.when(t < CTX_TILE)
        def _():
            run(0, STREAM)

        @pl.when(t == CTX_TILE)
        def _():
            run(SEQ, STREAM)
    else:
        run(0, STREAM)


def _attn_call(q_pad, kt, v, with_ctx):
    nt = N_TILES if with_ctx else N_LAT_TILES
    return pl.pallas_call(
        functools.partial(_attn_kernel, with_ctx=with_ctx),
        grid=(BATCH, nt),
        in_specs=[
            pl.BlockSpec((None, TILE, ATTN_HEADS * LANES), lambda b, t: (b, t, 0)),
            pl.BlockSpec((None, LANES, STREAM), lambda b, t: (b, 0, 0)),
            pl.BlockSpec((None, STREAM, LANES), lambda b, t: (b, 0, 0)),
        ],
        out_specs=pl.BlockSpec((None, TILE, ATTN_WIDTH), lambda b, t: (b, t, 0)),
        out_shape=jax.ShapeDtypeStruct((BATCH, STREAM, ATTN_WIDTH), BF16),
        compiler_params=_params("arbitrary", "arbitrary"),
        name="attention",
    )(q_pad, kt, v)


def _gdn_kernel(gq_ref, gg_ref, bd_ref, cw_ref, alog_ref, dtb_ref, ng_ref, seg_ref, tril_ref, triu_ref, ones_ref,
                eg_ref, eb_ref, o_ref, q_s, k_s, v_s, g_s, b_s, u_s, w_s, qk_s, qd_s, kd_s, ge_s, st_s, o_s):
    C = GDN_CHUNK
    seg = seg_ref[...]

    def prep_tile(i, carry):
        r0 = pl.multiple_of(i * TILE, TILE)
        x = gq_ref[pl.ds(r0, TILE), :].astype(F32)
        p0 = pl.multiple_of(jnp.maximum(r0 - 16, 0), 16)
        n0 = pl.multiple_of(jnp.minimum(r0 + TILE, STREAM - 16), 16)
        first_of_segment = jnp.logical_or(i == 0, i == CTX_TILE)
        last_of_segment = jnp.logical_or(i == N_LAT_TILES - 1, i == CTX_TILE)
        xp = gq_ref[pl.ds(p0, 16), :].astype(F32)[15:16, :] * jnp.where(first_of_segment, 0.0, 1.0)
        xn = gq_ref[pl.ds(n0, 16), :].astype(F32)[0:1, :] * jnp.where(last_of_segment, 0.0, 1.0)
        row = lax.broadcasted_iota(jnp.int32, (TILE, GDN_QKV_WIDTH), 0)
        xdn = jnp.where(row == 0, xp, pltpu.roll(x, 1, 0))
        xup = jnp.where(row == TILE - 1, xn, pltpu.roll(x, TILE - 1, 0))
        c = _silu(cw_ref[0:1, :] * xdn + cw_ref[1:2, :] * x + cw_ref[2:3, :] * xup)
        q = c[:, 0:256]
        k = c[:, 256:512]
        q = q * lax.rsqrt(_seg_sum(q * q, seg) + NORM_EPS)
        k = k * lax.rsqrt(_seg_sum(k * k, seg) + NORM_EPS)
        q_s[pl.ds(r0, TILE), :] = q * (GDN_DK ** -0.5)
        k_s[pl.ds(r0, TILE), :] = k
        v_s[pl.ds(r0, TILE), :] = c[:, 512:768]
        bd = bd_ref[pl.ds(r0, TILE), :]
        b_s[pl.ds(r0, TILE), :] = _sigmoid(bd)
        g_s[pl.ds(r0, TILE), :] = -jnp.exp(alog_ref[...]) * _softplus(bd + dtb_ref[...])
        return carry

    lax.fori_loop(0, N_TILES, prep_tile, 0)

    G = GDN_GROUP
    R = G * C
    ri = lax.broadcasted_iota(jnp.int32, (C, GDN_WIDTH), 0)
    cj = lax.broadcasted_iota(jnp.int32, (C, GDN_WIDTH), 1) % C
    lane = lax.broadcasted_iota(jnp.int32, (R, LANES), 1)
    eye4 = jnp.where(ri == cj, 1.0, 0.0)
    same16 = (ri // 16) == (cj // 16)
    off32 = jnp.logical_and((ri // 32) == (cj // 32), jnp.logical_not(same16))
    off64 = (ri // 32) != (cj // 32)
    incl = (ri >= cj, ri <= cj)
    strict = (ri > cj, ri < cj)

    def block_diag(z):
        zb = z.astype(BF16)
        return jnp.concatenate([zb, zb, zb, zb], axis=0) * seg

    def phase1(it, carry):
        c0 = it * G
        r0 = pl.multiple_of(c0 * C, R)
        kc = k_s[pl.ds(r0, R), :]
        qc = q_s[pl.ds(r0, R), :]
        vc = v_s[pl.ds(r0, R), :]
        gt = g_s[pl.ds(r0, R), :]
        bt = b_s[pl.ds(r0, R), :]
        g_fwd = _tri_dot(tril_ref[...], gt)
        g_bwd = _tri_dot(triu_ref[...], gt)
        g_tot = _tri_dot(ones_ref[...], gt)
        gc = jnp.where(lane < 8 + GDN_HEADS, g_fwd, g_bwd)
        gx = _dot_split3(gc, eg_ref[...])
        gtx = _dot_split3(g_tot, eg_ref[...])
        bx = _dot_split3(bt, eb_ref[...])
        e_g = jnp.exp(gx)
        e_gk = jnp.exp(gtx - gx)
        eye_rows = jnp.concatenate([jnp.concatenate([eye4, eye4], axis=1)] * G, axis=0)
        grow = _tri_dot(ones_ref[...], jnp.where(eye_rows > 0.0, gx, 0.0))
        kb = kc.astype(BF16)
        qb = qc.astype(BF16)
        kk, qk0 = [], []
        for g in range(G):
            rs = slice(g * C, (g + 1) * C)
            bdk = jnp.concatenate([kb[rs]] * 4, axis=0) * seg
            kk.append(_dot_nt(kb[rs], bdk))
            qk0.append(_dot_nt(qb[rs], bdk))
        probs = [(g, d) for g in range(G) for d in range(2)]

        def part(arr, g, d):
            return arr[g * C:(g + 1) * C, d * GDN_WIDTH:(d + 1) * GDN_WIDTH]

        decay = [jnp.where(incl[d], jnp.exp(part(gx, g, d) - part(grow, g, d)), 0.0) for g, d in probs]
        a = [jnp.where(strict[d], part(bx, g, d) * kk[g] * dec, 0.0) for (g, d), dec in zip(probs, decay)]
        x = [jnp.where(same16, -ai, 0.0) for ai in a]
        inv = [eye4 + xi for xi in x]
        y = [_dot(xi.astype(BF16), block_diag(xi)) for xi in x]
        for stage in range(3):
            if stage < 2:
                prod = [_dot(yi.astype(BF16), jnp.concatenate([block_diag(ii), block_diag(yi)], axis=1))
                        for yi, ii in zip(y, inv)]
                inv = [ii + pi[:, 0:GDN_WIDTH] for ii, pi in zip(inv, prod)]
                y = [pi[:, GDN_WIDTH:] for pi in prod]
            else:
                inv = [ii + _dot(yi.astype(BF16), block_diag(ii)) for yi, ii in zip(y, inv)]
        for off in (off32, off64):
            t = [_dot(ii.astype(BF16), block_diag(jnp.where(off, ai, 0.0))) for ii, ai in zip(inv, a)]
            inv = [ii - _dot(ti.astype(BF16), block_diag(ii)) for ii, ti in zip(inv, t)]
        for (g, d), ii, dec in zip(probs, inv, decay):
            rs = slice(g * C, (g + 1) * C)
            bxi = part(bx, g, d)
            egi = part(e_g, g, d)
            rhs = jnp.concatenate([block_diag(vc[rs] * bxi), block_diag(kc[rs] * (bxi * egi))], axis=1)
            r = _dot(ii.astype(BF16), rhs)
            u_s[c0 + g, d] = r[:, 0:GDN_WIDTH]
            w_s[c0 + g, d] = r[:, GDN_WIDTH:].astype(BF16)
            qk_s[c0 + g, d] = (qk0[g] * dec).astype(BF16)
            qd_s[c0 + g, d] = (qc[rs] * egi).astype(BF16)
            kd_s[c0 + g, d] = (kc[rs] * part(e_gk, g, d)).astype(BF16)
            ge_s[c0 + g, d] = jnp.exp(part(gtx, g, d)[0:8, :])
        return carry

    lax.fori_loop(0, N_CHUNKS // G, phase1, 0)

    st_s[...] = jnp.zeros(st_s.shape, F32)
    o_s[...] = jnp.zeros(o_s.shape, F32)
    bi = lax.broadcasted_iota(jnp.int32, (GDN_WIDTH, GDN_WIDTH), 0) // C
    bj = lax.broadcasted_iota(jnp.int32, (GDN_WIDTH, GDN_WIDTH), 1) // C
    on_diag = bi == bj

    def phase2(s, carry):
        cs = (jnp.where(s < N_CTX_CHUNKS, s + N_LAT_CHUNKS, s - N_CTX_CHUNKS), N_CHUNKS - 1 - s)
        st = [st_s[d] for d in range(2)]
        stb = [x.astype(BF16) for x in st]
        r1 = [_dot(jnp.concatenate([w_s[cs[d], d], qd_s[cs[d], d]], axis=0), stb[d]) for d in range(2)]
        vn = [u_s[cs[d], d] - r1[d][0:C, :] for d in range(2)]
        vnb = [x.astype(BF16) for x in vn]
        o = [r1[d][C:2 * C, :] + _dot(qk_s[cs[d], d], jnp.concatenate([vnb[d]] * 4, axis=0) * seg)
             for d in range(2)]
        upd = [_dot_tn(kd_s[cs[d], d], vnb[d]) for d in range(2)]
        for d in range(2):
            st_s[d] = st[d] * ge_s[cs[d], d][0:1, :] + jnp.where(on_diag, upd[d], 0.0)
            r0 = pl.multiple_of(cs[d] * C, C)
            o_s[pl.ds(r0, C), :] += o[d]
        return carry

    lax.fori_loop(0, N_CHUNKS, phase2, 0)

    def finish_tile(i, carry):
        r0 = pl.multiple_of(i * TILE, TILE)
        o = o_s[pl.ds(r0, TILE), :]
        ms = _seg_sum(o * o, seg) * (1.0 / GDN_DK)
        y = o * lax.rsqrt(ms + NORM_EPS) * ng_ref[...]
        o_ref[pl.ds(r0, TILE), :] = (y * _silu(gg_ref[pl.ds(r0, TILE), :].astype(F32))).astype(BF16)
        return carry

    lax.fori_loop(0, N_TILES, finish_tile, 0)


def _gdn_call(gqkv, ggate, bd, conv_w, alog_row, dtb_row, ng_row, seg, tril, triu, ones, expand_g, expand_b):
    def per_batch(width):
        return pl.BlockSpec((None, STREAM, width), lambda b: (b, 0, 0), pipeline_mode=pl.Buffered(1))
    rows = GDN_GROUP * GDN_CHUNK
    packed = (N_CHUNKS, 2, GDN_CHUNK, GDN_WIDTH)
    return pl.pallas_call(
        _gdn_kernel,
        grid=(BATCH,),
        in_specs=[
            per_batch(768), per_batch(256), per_batch(LANES),
            _const_spec((3, 768)), _const_spec((1, LANES)), _const_spec((1, LANES)), _const_spec((1, GDN_WIDTH)),
            _const_spec((GDN_WIDTH, GDN_WIDTH)), _const_spec((rows, rows)), _const_spec((rows, rows)),
            _const_spec((rows, rows)), _const_spec((LANES, 2 * GDN_WIDTH)), _const_spec((LANES, 2 * GDN_WIDTH)),
        ],
        out_specs=pl.BlockSpec((None, STREAM, GDN_WIDTH), lambda b: (b, 0, 0)),
        out_shape=jax.ShapeDtypeStruct((BATCH, STREAM, GDN_WIDTH), BF16),
        scratch_shapes=[
            pltpu.VMEM((STREAM, GDN_WIDTH), F32),
            pltpu.VMEM((STREAM, GDN_WIDTH), F32),
            pltpu.VMEM((STREAM, GDN_WIDTH), F32),
            pltpu.VMEM((STREAM, LANES), F32),
            pltpu.VMEM((STREAM, LANES), F32),
            pltpu.VMEM(packed, F32),
            pltpu.VMEM(packed, BF16),
            pltpu.VMEM(packed, BF16),
            pltpu.VMEM(packed, BF16),
            pltpu.VMEM(packed, BF16),
            pltpu.VMEM((N_CHUNKS, 2, 8, GDN_WIDTH), F32),
            pltpu.VMEM((2, GDN_WIDTH, GDN_WIDTH), F32),
            pltpu.VMEM((STREAM, GDN_WIDTH), F32),
        ],
        compiler_params=_params("arbitrary"),
        name="gdn",
    )(gqkv, ggate, bd, conv_w, alog_row, dtb_row, ng_row, seg, tril, triu, ones, expand_g, expand_b)


def _hyena_filter_kernel(feat_ref, w1_ref, b1_ref, w2_ref, b2_ref, w3_ref, fr_ref, win_ref, fc_ref,
                         p_ref, q_ref, r_ref, *, n):
    fr = fr_ref[...]
    hid = jnp.sin(fr * (_dot(feat_ref[...].astype(BF16), w1_ref[...]) + b1_ref[...]))
    hid = jnp.sin(fr * (_dot(hid.astype(BF16), w2_ref[...]) + b2_ref[...]))
    filt = _dot(hid.astype(BF16), w3_ref[...])
    win = win_ref[...]
    row = lax.broadcasted_iota(jnp.int32, (n, HYENA_WIDTH), 0)
    h_fwd = filt[:, 0:HYENA_WIDTH] * win
    h_bwd = jnp.where(row == 0, 0.0, filt[:, HYENA_WIDTH:] * win)
    hcat = jnp.concatenate([h_fwd, h_bwd], axis=1).astype(BF16)
    h_re = _dot(fc_ref[:, 0:n], hcat)
    h_im = _dot(fc_ref[:, n:2 * n], hcat)
    sgn = jnp.where((row & 1) == 0, 1.0, -1.0)
    nyq = jnp.sum(sgn * (h_fwd + h_bwd), axis=0, keepdims=True)
    re = h_re[:, 0:HYENA_WIDTH] + h_re[:, HYENA_WIDTH:]
    p_ref[...] = re
    q_ref[...] = h_im[:, 0:HYENA_WIDTH] - h_im[:, HYENA_WIDTH:]
    r_ref[...] = jnp.where(row == 0, nyq, re)


def _hyena_filter_call(n, feats, w1, b1, w2, b2, w3, fr, win, fcat):
    shapes = [(n, LANES), (LANES, HYENA_FILTER_HIDDEN), (1, HYENA_FILTER_HIDDEN),
              (HYENA_FILTER_HIDDEN, HYENA_FILTER_HIDDEN), (1, HYENA_FILTER_HIDDEN),
              (HYENA_FILTER_HIDDEN, 2 * HYENA_WIDTH), (1, HYENA_FILTER_HIDDEN), (n, HYENA_WIDTH), (n, 2 * n)]
    out = jax.ShapeDtypeStruct((n, HYENA_WIDTH), F32)
    return pl.pallas_call(
        functools.partial(_hyena_filter_kernel, n=n),
        grid=(1,),
        in_specs=[_const_spec(s) for s in shapes],
        out_specs=[pl.BlockSpec((n, HYENA_WIDTH), lambda i: (0, 0))] * 3,
        out_shape=[out, out, out],
        compiler_params=_params("arbitrary"),
        name=f"hyena_filter_{n}",
    )(feats, w1, b1, w2, b2, w3, fr, win, fcat)


def _hyena_kernel(x_ref, cw_ref, bias_ref, fc_ref, p_ref, q_ref, r_ref, o_ref, w_s, x0_s, z_s, *, n):
    rows = min(n, TILE)
    n_tiles = n // rows
    inv_n = 1.0 / (2 * n)

    def conv_tile(i, carry):
        r0 = pl.multiple_of(i * rows, rows)
        x = x_ref[pl.ds(r0, rows), :].astype(F32)
        p0 = pl.multiple_of(jnp.maximum(r0 - 16, 0), 16)
        n0 = pl.multiple_of(jnp.minimum(r0 + rows, n - 16), 16)
        xp = x_ref[pl.ds(p0, 16), :].astype(F32)[15:16, :] * jnp.where(i == 0, 0.0, 1.0)
        xn = x_ref[pl.ds(n0, 16), :].astype(F32)[0:1, :] * jnp.where(i == n_tiles - 1, 0.0, 1.0)
        row = lax.broadcasted_iota(jnp.int32, (rows, 3 * HYENA_WIDTH), 0)
        xdn = jnp.where(row == 0, xp, pltpu.roll(x, 1, 0))
        xup = jnp.where(row == rows - 1, xn, pltpu.roll(x, rows - 1, 0))
        c = cw_ref[0:1, :] * xdn + cw_ref[1:2, :] * x + cw_ref[2:3, :] * xup
        x0_s[pl.ds(r0, rows), :] = c[:, 0:HYENA_WIDTH]
        w_s[pl.ds(r0, rows), :] = c[:, 2 * HYENA_WIDTH:] * c[:, HYENA_WIDTH:2 * HYENA_WIDTH]
        return carry

    lax.fori_loop(0, n_tiles, conv_tile, 0)

    w = w_s[...]
    wb = w.astype(BF16)
    u_re = _dot(fc_ref[:, 0:n], wb)
    u_im = _dot(fc_ref[:, n:2 * n], wb)
    row = lax.broadcasted_iota(jnp.int32, (n, HYENA_WIDTH), 0)
    sgn = jnp.where((row & 1) == 0, 1.0, -1.0)
    u_nyq = jnp.sum(sgn * w, axis=0, keepdims=True)
    u_im = jnp.where(row == 0, u_nyq, u_im)
    fscale = jnp.where(row == 0, inv_n, 2.0 * inv_n)
    z_re = u_re * p_ref[...] - u_im * q_ref[...]
    z_im = u_re * q_ref[...] + u_im * r_ref[...]
    z_s[0:n, :] = (z_re * fscale).astype(BF16)
    z_s[n:2 * n, :] = (z_im * fscale).astype(BF16)
    y = _dot(fc_ref[...], z_s[...]) + sgn * (z_im[0:1, :] * inv_n) + w * bias_ref[...]
    o_ref[...] = (y * x0_s[...]).astype(BF16)


def _hyena_call(n, tile_index, hy, conv_w, bias, fcat, hp, hq, hr):
    return pl.pallas_call(
        functools.partial(_hyena_kernel, n=n),
        grid=(BATCH,),
        in_specs=[
            pl.BlockSpec((None, n, 3 * HYENA_WIDTH), lambda b: (b, tile_index, 0)),
            _const_spec((3, 3 * HYENA_WIDTH)),
            _const_spec((1, HYENA_WIDTH)),
            _const_spec((n, 2 * n)),
            _const_spec((n, HYENA_WIDTH)), _const_spec((n, HYENA_WIDTH)), _const_spec((n, HYENA_WIDTH)),
        ],
        out_specs=pl.BlockSpec((None, n, HYENA_WIDTH), lambda b: (b, 0, 0)),
        out_shape=jax.ShapeDtypeStruct((BATCH, n, HYENA_WIDTH), BF16),
        scratch_shapes=[
            pltpu.VMEM((n, HYENA_WIDTH), F32),
            pltpu.VMEM((n, HYENA_WIDTH), F32),
            pltpu.VMEM((2 * n, HYENA_WIDTH), BF16),
        ],
        compiler_params=_params("arbitrary"),
        name=f"hyena_{n}",
    )(hy, conv_w, bias, fcat, hp, hq, hr)


def _out_kernel(h_ref, a_ref, g_ref, yl_ref, yc_ref, m_ref, w_ref, o_ref, *, with_ctx):
    if with_ctx:
        y = jnp.where(pl.program_id(1) == CTX_TILE, yc_ref[...], yl_ref[...])
    else:
        y = yl_ref[...]
    acc = _dot(a_ref[...], w_ref[0:512, :])
    acc = acc + _dot(g_ref[...], w_ref[512:768, :])
    acc = acc + _dot(y, w_ref[768:1024, :])
    o_ref[...] = h_ref[...] + m_ref[2:3, :] * acc


def _out_call(h, attn, gdn, hy_lat, hy_ctx, mod, w_out, with_ctx):
    nt = N_TILES if with_ctx else N_LAT_TILES

    def tile(width):
        return pl.BlockSpec((None, TILE, width), lambda b, t: (b, t, 0))
    return pl.pallas_call(
        functools.partial(_out_kernel, with_ctx=with_ctx),
        grid=(BATCH, nt),
        in_specs=[
            tile(D_MODEL), tile(ATTN_WIDTH), tile(GDN_WIDTH),
            pl.BlockSpec((None, TILE, HYENA_WIDTH), lambda b, t: (b, jnp.minimum(t, N_LAT_TILES - 1), 0)),
            pl.BlockSpec((None, TILE, HYENA_WIDTH), lambda b, t: (b, 0, 0)),
            pl.BlockSpec((None, 6, D_MODEL), lambda b, t: (_mod_row(b, t), 0, 0)),
            _const_spec((D_MODEL, D_MODEL)),
        ],
        out_specs=tile(D_MODEL),
        out_shape=jax.ShapeDtypeStruct((BATCH, STREAM, D_MODEL), F32),
        input_output_aliases={0: 0},
        compiler_params=_params("arbitrary", "arbitrary"),
        name="out_proj",
    )(h, attn, gdn, hy_lat, hy_ctx, mod, w_out)


def _ffn_kernel(h_ref, hp_ref, hn_ref, m_ref, g_ref, wu_ref, cw_ref, wd_ref, o_ref, acc_s):
    t = pl.program_id(1)
    ext = TILE + 2 * HALO
    x = jnp.concatenate([hp_ref[...], h_ref[...], hn_ref[...]], axis=0)
    u = _norm_mod(x, g_ref[...], m_ref[3:4, :], m_ref[4:5, :]).astype(BF16)
    first_of_segment = jnp.logical_or(t == 0, t == CTX_TILE)
    last_of_segment = jnp.logical_or(t == N_LAT_TILES - 1, t == CTX_TILE)
    row = lax.broadcasted_iota(jnp.int32, (ext, 2 * FF_CHUNK), 0)
    keep = jnp.where(row < HALO, jnp.where(first_of_segment, 0.0, 1.0),
                     jnp.where(row >= HALO + TILE, jnp.where(last_of_segment, 0.0, 1.0), 1.0))
    acc_s[...] = jnp.zeros(acc_s.shape, F32)

    def chunk(j, carry):
        p = _dot(u, wu_ref[j]) * keep
        cw = cw_ref[j]
        c = (cw[0:1, :] * pltpu.roll(p, 1, 0) + cw[1:2, :] * p + cw[2:3, :] * pltpu.roll(p, ext - 1, 0))
        c = c[HALO:HALO + TILE, :]
        act = (_silu(c[:, 0:FF_CHUNK]) * c[:, FF_CHUNK:]).astype(BF16)
        acc_s[...] += _dot(act, wd_ref[j])
        return carry

    lax.fori_loop(0, N_FF_CHUNKS, chunk, 0)
    o_ref[...] = h_ref[...] + m_ref[5:6, :] * acc_s[...]


def _ffn_call(h, mod, gain, w_up, conv_w, w_down, with_ctx):
    nt = N_TILES if with_ctx else N_LAT_TILES
    per = TILE // HALO
    last = STREAM // HALO - 1
    return pl.pallas_call(
        _ffn_kernel,
        grid=(BATCH, nt),
        in_specs=[
            pl.BlockSpec((None, TILE, D_MODEL), lambda b, t: (b, t, 0)),
            pl.BlockSpec((None, HALO, D_MODEL), lambda b, t: (b, jnp.maximum(t * per - 1, 0), 0)),
            pl.BlockSpec((None, HALO, D_MODEL), lambda b, t: (b, jnp.minimum((t + 1) * per, last), 0)),
            pl.BlockSpec((None, 6, D_MODEL), lambda b, t: (_mod_row(b, t), 0, 0)),
            _const_spec((1, D_MODEL)),
            _const_spec((N_FF_CHUNKS, D_MODEL, 2 * FF_CHUNK)),
            _const_spec((N_FF_CHUNKS, 8, 2 * FF_CHUNK)),
            _const_spec((N_FF_CHUNKS, FF_CHUNK, D_MODEL)),
        ],
        out_specs=pl.BlockSpec((None, TILE, D_MODEL), lambda b, t: (b, t, 0)),
        out_shape=jax.ShapeDtypeStruct((BATCH, STREAM, D_MODEL), F32),
        scratch_shapes=[pltpu.VMEM((TILE, D_MODEL), F32)],
        compiler_params=_params("arbitrary", "arbitrary"),
        name="conv_ffn",
    )(h, h, h, mod, gain, w_up, conv_w, w_down)


def _final_kernel(h_ref, g_ref, o_ref):
    x = h_ref[...]
    ms = jnp.mean(x * x, axis=-1, keepdims=True)
    o_ref[...] = x * lax.rsqrt(ms + NORM_EPS) * g_ref[...]


def _final_call(h, gain):
    return pl.pallas_call(
        _final_kernel,
        grid=(BATCH, N_LAT_TILES),
        in_specs=[pl.BlockSpec((None, TILE, D_MODEL), lambda b, t: (b, t, 0)), _const_spec((1, D_MODEL))],
        out_specs=pl.BlockSpec((None, TILE, D_MODEL), lambda b, t: (b, t, 0)),
        out_shape=jax.ShapeDtypeStruct((BATCH, SEQ, D_MODEL), F32),
        compiler_params=_params("arbitrary", "arbitrary"),
        name="final_norm",
    )(h, gain)


def _block_diag_ones(width, block):
    i = jnp.arange(width)
    return (i[:, None] // block == i[None, :] // block).astype(BF16)


def _rope_tables():
    n = jnp.arange(SEQ)
    lane = jnp.arange(LANES)
    d = lane % HEAD_DIM
    comp = jnp.where((d < HEAD_DIM // 2)[None, :], (n // GRID_W)[:, None], (n % GRID_W)[:, None]).astype(F32)
    inv = ROPE_THETA ** (-(d % 16).astype(F32) / 16.0)
    ang = comp * inv[None, :]
    sign = jnp.where((d % 32) < 16, -1.0, 1.0)[None, :]
    cos_t = jnp.concatenate([jnp.cos(ang), jnp.ones((CTX_LEN, LANES), F32)], axis=0)
    sin_t = jnp.concatenate([jnp.sin(ang) * sign, jnp.zeros((CTX_LEN, LANES), F32)], axis=0)
    return cos_t, sin_t


def _dft_table(n):
    f = jnp.arange(n, dtype=jnp.int32)
    m = (f[:, None] * f[None, :]) % (2 * n)
    ang = m.astype(F32) * (math.pi / n)
    return jnp.concatenate([jnp.cos(ang), -jnp.sin(ang)], axis=1).astype(BF16)


def _hyena_static(n):
    t = jnp.linspace(0.0, 1.0, n, dtype=F32)[:, None]
    omega = 2.0 * math.pi * jnp.arange(n, dtype=F32)[:, None] / n
    bands = jnp.linspace(1e-4, HYENA_EMB_BANDS - 1, HYENA_EMB_BANDS, dtype=F32)[None, :]
    feats = jnp.concatenate([t, jnp.cos(bands * omega), -jnp.sin(bands * omega)], axis=-1)
    feats = jnp.pad(feats, ((0, 0), (0, LANES - feats.shape[1])))
    max_decay = math.log(HYENA_DECAY_TARGET) / HYENA_FAST_DECAY
    min_decay = math.log(HYENA_DECAY_TARGET) / HYENA_SLOW_DECAY
    deltas = jnp.abs(jnp.linspace(min_decay, max_decay, HYENA_WIDTH, dtype=F32))
    window = jnp.exp(-t * deltas[None, :])
    return feats, window


def kernel(x, c, ctx, c_ctx, w_ada, b_ada, norm1, norm2, w_in, w_out, q_gain, k_gain, gdn_conv, gdn_a_log,
           gdn_dt_bias, gdn_norm, hyena_conv, hyena_w1, hyena_b1, hyena_w2, hyena_b2, hyena_w3, hyena_freq,
           hyena_bias, ffn_up, ffn_conv, ffn_down, final_norm):
    assert x.shape == (BATCH, SEQ, D_MODEL) and ctx.shape == (BATCH, CTX_LEN, D_MODEL)

    h = jnp.concatenate([x, ctx], axis=1)
    cvec = jnp.concatenate([c, c_ctx[None, :], jnp.zeros((MOD_ROWS - BATCH - 1, D_MODEL), F32)], axis=0)
    mod_all = _ada_call(cvec, w_ada, b_ada).reshape(DEPTH, MOD_ROWS, 6, D_MODEL)

    cos_t, sin_t = _rope_tables()
    seg_head = _block_diag_ones(LANES, HEAD_DIM)
    seg_gdn = _block_diag_ones(GDN_WIDTH, GDN_DK)
    ii = jnp.arange(GDN_GROUP * GDN_CHUNK)
    same_chunk = (ii[:, None] // GDN_CHUNK) == (ii[None, :] // GDN_CHUNK)
    tril = jnp.logical_and(same_chunk, ii[:, None] >= ii[None, :]).astype(BF16)
    triu = jnp.logical_and(same_chunk, ii[:, None] <= ii[None, :]).astype(BF16)
    chunk_ones = same_chunk.astype(BF16)
    col = jnp.arange(2 * GDN_WIDTH) // GDN_DK
    expand_g = (jnp.arange(LANES)[:, None] == 8 + col[None, :]).astype(BF16)
    expand_b = (jnp.arange(LANES)[:, None] == col[None, :]).astype(BF16)
    fcat_lat = _dft_table(SEQ)
    fcat_ctx = _dft_table(CTX_LEN)
    feats_lat, win_lat = _hyena_static(SEQ)
    feats_ctx, win_ctx = _hyena_static(CTX_LEN)

    for i in range(DEPTH):
        with_ctx = i < DEPTH - 1
        mod = mod_all[i]
        wi = w_in[i]
        w_cat = jnp.concatenate(
            [wi[:, 0:768], wi[:, 768:1536], wi[:, 1536:1792], wi[:, 1808:2576], wi[:, 1792:1808],
             jnp.zeros((D_MODEL, LANES - 16), F32)], axis=1).astype(BF16)
        aqkv, gqkv, ggate, hyp, bd = _in_call(h, mod, norm1[i][None, :], w_cat)

        q_pad, kt, v = _attn_prep_call(aqkv, cos_t, sin_t, jnp.tile(q_gain[i], 2)[None, :],
                                       jnp.tile(k_gain[i], 2)[None, :], seg_head)
        attn = _attn_call(q_pad, kt, v, with_ctx)

        pad8 = jnp.zeros((8,), F32)
        alog_row = jnp.concatenate([pad8, gdn_a_log[i].reshape(-1), jnp.zeros((LANES - 16,), F32)])[None, :]
        dtb_row = jnp.concatenate([pad8, gdn_dt_bias[i].reshape(-1), jnp.zeros((LANES - 16,), F32)])[None, :]
        gdn = _gdn_call(gqkv, ggate, bd, gdn_conv[i], alog_row, dtb_row, jnp.tile(gdn_norm[i], GDN_HEADS)[None, :],
                        seg_gdn, tril, triu, chunk_ones, expand_g, expand_b)

        w1 = jnp.pad(hyena_w1[i], ((0, LANES - hyena_w1.shape[1]), (0, 0))).astype(BF16)
        filt_args = (w1, hyena_b1[i][None, :], hyena_w2[i].astype(BF16), hyena_b2[i][None, :],
                     hyena_w3[i].astype(BF16), hyena_freq[i][None, :])
        hp, hq, hr = _hyena_filter_call(SEQ, feats_lat, *filt_args, win_lat, fcat_lat)
        hy_lat = _hyena_call(SEQ, 0, hyp, hyena_conv[i], hyena_bias[i], fcat_lat, hp, hq, hr)
        if with_ctx:
            hp, hq, hr = _hyena_filter_call(CTX_LEN, feats_ctx, *filt_args, win_ctx, fcat_ctx)
            hy_ctx = _hyena_call(CTX_LEN, CTX_TILE, hyp, hyena_conv[i], hyena_bias[i], fcat_ctx, hp, hq, hr)
        else:
            hy_ctx = hy_lat

        h = _out_call(h, attn, gdn, hy_lat, hy_ctx, mod, w_out[i].astype(BF16), with_ctx)

        wu = ffn_up[i]
        w_up = jnp.concatenate([wu[:, 0:D_FF].reshape(D_MODEL, N_FF_CHUNKS, FF_CHUNK),
                                wu[:, D_FF:].reshape(D_MODEL, N_FF_CHUNKS, FF_CHUNK)], axis=2)
        w_up = jnp.transpose(w_up, (1, 0, 2)).astype(BF16)
        fc = ffn_conv[i]
        conv_w = jnp.concatenate([fc[:, 0:D_FF].reshape(3, N_FF_CHUNKS, FF_CHUNK),
                                  fc[:, D_FF:].reshape(3, N_FF_CHUNKS, FF_CHUNK)], axis=2)
        conv_w = jnp.pad(jnp.transpose(conv_w, (1, 0, 2)), ((0, 0), (0, 5), (0, 0)))
        w_down = ffn_down[i].astype(BF16).reshape(N_FF_CHUNKS, FF_CHUNK, D_MODEL)
        h = _ffn_call(h, mod, norm2[i][None, :], w_up, conv_w, w_down, with_ctx)

    return _final_call(h, final_norm[None, :])
```

```python
import functools
import math

import jax
import jax.numpy as jnp
from jax import lax
from jax.experimental import pallas as pl
from jax.experimental.pallas import tpu as pltpu

F32 = jnp.float32
BF16 = jnp.bfloat16

D_MODEL = 1024
BATCH = 16
SEQ = 2048
DEPTH = 4
CTX_LEN = 256
STREAM = SEQ + CTX_LEN
TILE = 256
N_TILES = STREAM // TILE
N_LAT_TILES = SEQ // TILE
CTX_TILE = N_LAT_TILES
GRID_W = 64
NORM_EPS = 1e-6
HEAD_DIM = 64
LANES = 128
MOD_ROWS = 24
CTX_MOD_ROW = BATCH

ATTN_WIDTH = 512
ATTN_HEADS = 8
ATTN_KV_WIDTH = 128
ROPE_THETA = 10000.0

GDN_WIDTH = 256
GDN_HEADS = 4
GDN_DK = 64
GDN_QKV_WIDTH = 768
GDN_CHUNK = 64
N_CHUNKS = STREAM // GDN_CHUNK
N_LAT_CHUNKS = SEQ // GDN_CHUNK
N_CTX_CHUNKS = CTX_LEN // GDN_CHUNK
GDN_GROUP = 4

HYENA_WIDTH = 256
HYENA_EMB_BANDS = 16
HYENA_FILTER_HIDDEN = 64
HYENA_FAST_DECAY = 0.3
HYENA_SLOW_DECAY = 1.5
HYENA_DECAY_TARGET = 1e-2

D_FF = 2816
FF_CHUNK = 256
N_FF_CHUNKS = D_FF // FF_CHUNK
HALO = 8
FFN_ROWS = 512

COL_ATTN = (0, 768)
COL_GQKV = (768, 1536)
COL_GGATE = (1536, 1792)
COL_HY = (1792, 2560)
COL_BD = (2560, 2688)
IN_COLS = 2688

VMEM_LIMIT = 56 * 1024 * 1024


def _dot(a, b):
    return jnp.dot(a, b, preferred_element_type=F32)


def _dot_nt(a, b):
    return lax.dot_general(a, b, (((1,), (1,)), ((), ())), preferred_element_type=F32)


def _dot_tn(a, b):
    return lax.dot_general(a, b, (((0,), (0,)), ((), ())), preferred_element_type=F32)


def _sigmoid(x):
    return 1.0 / (1.0 + jnp.exp(-x))


def _silu(x):
    return x * _sigmoid(x)


def _softplus(x):
    return jnp.maximum(x, 0.0) + jnp.log(1.0 + jnp.exp(-jnp.abs(x)))


def _seg_sum(x, seg):
    hi = x.astype(BF16)
    lo = (x - hi.astype(F32)).astype(BF16)
    return _dot(hi, seg) + _dot(lo, seg)


def _tri_dot(tri, x):
    x1 = x.astype(BF16)
    r1 = x - x1.astype(F32)
    x2 = r1.astype(BF16)
    x3 = (r1 - x2.astype(F32)).astype(BF16)
    return _dot(tri, x1) + _dot(tri, x2) + _dot(tri, x3)


def _dot_split(x, sel, terms):
    out = None
    for _ in range(terms):
        piece = x.astype(BF16)
        x = x - piece.astype(F32)
        out = _dot(piece, sel) if out is None else out + _dot(piece, sel)
    return out


def _params(*sem):
    return pltpu.CompilerParams(dimension_semantics=sem, vmem_limit_bytes=VMEM_LIMIT)


def _const_spec(shape):
    nd = len(shape)
    return pl.BlockSpec(shape, lambda *_: (0,) * nd, pipeline_mode=pl.Buffered(1))


def _mod_row(b, t):
    return jnp.where(t == CTX_TILE, CTX_MOD_ROW, b)


def _ada_kernel(c_ref, w_ref, b_ref, o_ref):
    s = _silu(c_ref[...]).astype(BF16)
    o_ref[...] = _dot(s, w_ref[...].astype(BF16)) + b_ref[...]


def _ada_call(cvec, w_ada, b_ada):
    nj = 6
    return pl.pallas_call(
        _ada_kernel,
        grid=(DEPTH, nj),
        in_specs=[
            pl.BlockSpec((MOD_ROWS, D_MODEL), lambda i, j: (0, 0)),
            pl.BlockSpec((None, D_MODEL, D_MODEL), lambda i, j: (i, 0, j)),
            pl.BlockSpec((None, 1, D_MODEL), lambda i, j: (i, 0, j)),
        ],
        out_specs=pl.BlockSpec((None, MOD_ROWS, D_MODEL), lambda i, j: (i, 0, j)),
        out_shape=jax.ShapeDtypeStruct((DEPTH, MOD_ROWS, 6 * D_MODEL), F32),
        compiler_params=_params("arbitrary", "arbitrary"),
        name="adaln",
    )(cvec, w_ada, b_ada.reshape(DEPTH, 1, 6 * D_MODEL))


def _norm_mod(x, gain, shift, scale):
    ms = jnp.mean(x * x, axis=-1, keepdims=True)
    return (x * lax.rsqrt(ms + NORM_EPS) * gain) * (1.0 + scale) + shift


def _in_kernel(xl_ref, xc_ref, m_ref, g_ref, w_ref, oa_ref, ogq_ref, ogg_ref, ohy_ref, obd_ref):
    x = jnp.where(pl.program_id(1) == CTX_TILE, xc_ref[...], xl_ref[...])
    u = _norm_mod(x, g_ref[...], m_ref[0:1, :], m_ref[1:2, :]).astype(BF16)
    oa_ref[...] = _dot(u, w_ref[:, COL_ATTN[0]:COL_ATTN[1]]).astype(BF16)
    ogq_ref[...] = _dot(u, w_ref[:, COL_GQKV[0]:COL_GQKV[1]]).astype(BF16)
    ogg_ref[...] = _dot(u, w_ref[:, COL_GGATE[0]:COL_GGATE[1]]).astype(BF16)
    ohy_ref[...] = _dot(u, w_ref[:, COL_HY[0]:COL_HY[1]]).astype(BF16)
    obd_ref[...] = _dot(u, w_ref[:, COL_BD[0]:COL_BD[1]])


def _in_call(h_lat, h_ctx, mod, gain, w_cat):
    def tile(width):
        return pl.BlockSpec((None, TILE, width), lambda b, t: (b, t, 0))
    return pl.pallas_call(
        _in_kernel,
        grid=(BATCH, N_TILES),
        in_specs=[
            pl.BlockSpec((None, TILE, D_MODEL), lambda b, t: (b, jnp.minimum(t, N_LAT_TILES - 1), 0)),
            pl.BlockSpec((None, TILE, D_MODEL), lambda b, t: (b, 0, 0)),
            pl.BlockSpec((None, 6, D_MODEL), lambda b, t: (_mod_row(b, t), 0, 0)),
            _const_spec((1, D_MODEL)),
            _const_spec((D_MODEL, IN_COLS)),
        ],
        out_specs=[tile(768), tile(768), tile(256), tile(768), tile(LANES)],
        out_shape=[
            jax.ShapeDtypeStruct((BATCH, STREAM, 768), BF16),
            jax.ShapeDtypeStruct((BATCH, STREAM, 768), BF16),
            jax.ShapeDtypeStruct((BATCH, STREAM, 256), BF16),
            jax.ShapeDtypeStruct((BATCH, STREAM, 768), BF16),
            jax.ShapeDtypeStruct((BATCH, STREAM, LANES), F32),
        ],
        compiler_params=_params("arbitrary", "arbitrary"),
        name="in_proj",
    )(h_lat, h_ctx, mod, gain, w_cat)


def _attn_prep_kernel(a_ref, cos_ref, sin_ref, qg_ref, kg_ref, seg_ref, q_ref, kt_ref, v_ref):
    lane = lax.broadcasted_iota(jnp.int32, (TILE, LANES), 1)
    first = (lane % 32) < 16
    low = lane < HEAD_DIM
    seg = seg_ref[...]
    cosv = cos_ref[...]
    sinv = sin_ref[...]

    def norm_rope(x, gain):
        ms = _seg_sum(x * x, seg) * (1.0 / HEAD_DIM)
        y = x * lax.rsqrt(ms + NORM_EPS) * gain
        partner = jnp.where(first, pltpu.roll(y, LANES - 16, 1), pltpu.roll(y, 16, 1))
        return y * cosv + partner * sinv

    for j in range(ATTN_HEADS // 2):
        r = norm_rope(a_ref[:, j * LANES:(j + 1) * LANES].astype(F32), qg_ref[...]) * (HEAD_DIM ** -0.5)
        swapped = pltpu.roll(r, HEAD_DIM, 1)
        if j // 2 == 0:
            qa = jnp.where(low, r, 0.0)
            qb = jnp.where(low, swapped, 0.0)
        else:
            qa = jnp.where(low, 0.0, swapped)
            qb = jnp.where(low, 0.0, r)
        q_ref[:, (2 * j) * LANES:(2 * j + 1) * LANES] = qa.astype(BF16)
        q_ref[:, (2 * j + 1) * LANES:(2 * j + 2) * LANES] = qb.astype(BF16)
    rk = norm_rope(a_ref[:, ATTN_WIDTH:ATTN_WIDTH + ATTN_KV_WIDTH].astype(F32), kg_ref[...])
    kt_ref[...] = rk.T.astype(BF16)
    v_ref[...] = a_ref[:, ATTN_WIDTH + ATTN_KV_WIDTH:]


def _attn_prep_call(aqkv, cos_t, sin_t, qg, kg, seg):
    return pl.pallas_call(
        _attn_prep_kernel,
        grid=(BATCH, N_TILES),
        in_specs=[
            pl.BlockSpec((None, TILE, 768), lambda b, t: (b, t, 0)),
            pl.BlockSpec((TILE, LANES), lambda b, t: (t, 0)),
            pl.BlockSpec((TILE, LANES), lambda b, t: (t, 0)),
            _const_spec((1, LANES)),
            _const_spec((1, LANES)),
            _const_spec((LANES, LANES)),
        ],
        out_specs=[
            pl.BlockSpec((None, TILE, ATTN_HEADS * LANES), lambda b, t: (b, t, 0)),
            pl.BlockSpec((None, LANES, TILE), lambda b, t: (b, 0, t)),
            pl.BlockSpec((None, TILE, LANES), lambda b, t: (b, t, 0)),
        ],
        out_shape=[
            jax.ShapeDtypeStruct((BATCH, STREAM, ATTN_HEADS * LANES), BF16),
            jax.ShapeDtypeStruct((BATCH, LANES, STREAM), BF16),
            jax.ShapeDtypeStruct((BATCH, STREAM, LANES), BF16),
        ],
        compiler_params=_params("arbitrary", "arbitrary"),
        name="attn_prep",
    )(aqkv, cos_t, sin_t, qg, kg, seg)


def _attn_kernel(q_ref, kt_ref, v_ref, o_ref, *, with_ctx):
    low = lax.broadcasted_iota(jnp.int32, (TILE, LANES), 1) < HEAD_DIM

    def run(k0, k1):
        kt = kt_ref[:, k0:k1]
        v = v_ref[k0:k1, :]
        for j in range(ATTN_HEADS // 2):
            outs = []
            for e in range(2):
                hq = 2 * j + e
                s = _dot(q_ref[:, hq * LANES:(hq + 1) * LANES], kt)
                p = jnp.exp(s - jnp.max(s, axis=-1, keepdims=True))
                den = jnp.sum(p, axis=-1, keepdims=True)
                outs.append(_dot(p.astype(BF16), v) / den)
            if j // 2 == 0:
                blk = jnp.where(low, outs[0], pltpu.roll(outs[1], HEAD_DIM, 1))
            else:
                blk = jnp.where(low, pltpu.roll(outs[0], HEAD_DIM, 1), outs[1])
            o_ref[:, j * LANES:(j + 1) * LANES] = blk.astype(BF16)

    if with_ctx:
        t = pl.program_id(1)

        @pl.when(t < CTX_TILE)
        def _():
            run(0, STREAM)

        @pl.when(t == CTX_TILE)
        def _():
            run(SEQ, STREAM)
    else:
        run(0, STREAM)


def _attn_call(q_pad, kt, v, with_ctx):
    nt = N_TILES if with_ctx else N_LAT_TILES
    return pl.pallas_call(
        functools.partial(_attn_kernel, with_ctx=with_ctx),
        grid=(BATCH, nt),
        in_specs=[
            pl.BlockSpec((None, TILE, ATTN_HEADS * LANES), lambda b, t: (b, t, 0)),
            pl.BlockSpec((None, LANES, STREAM), lambda b, t: (b, 0, 0)),
            pl.BlockSpec((None, STREAM, LANES), lambda b, t: (b, 0, 0)),
        ],
        out_specs=pl.BlockSpec((None, TILE, ATTN_WIDTH), lambda b, t: (b, t, 0)),
        out_shape=jax.ShapeDtypeStruct((BATCH, STREAM, ATTN_WIDTH), BF16),
        compiler_params=_params("arbitrary", "arbitrary"),
        name="attention",
    )(q_pad, kt, v)


def _gdn_kernel(gq_ref, gg_ref, bd_ref, cw_ref, alog_ref, dtb_ref, ng_ref, seg_ref, tril_ref, triu_ref, ones_ref,
                eg_ref, eb_ref, o_ref, q_s, k_s, v_s, g_s, b_s, u_s, w_s, qk_s, qd_s, kd_s, ge_s, st_s, o_s):
    C = GDN_CHUNK
    seg = seg_ref[...]

    def prep_tile(i, carry):
        r0 = pl.multiple_of(i * TILE, TILE)
        x = gq_ref[pl.ds(r0, TILE), :].astype(F32)
        p0 = pl.multiple_of(jnp.maximum(r0 - 16, 0), 16)
        n0 = pl.multiple_of(jnp.minimum(r0 + TILE, STREAM - 16), 16)
        first_of_segment = jnp.logical_or(i == 0, i == CTX_TILE)
        last_of_segment = jnp.logical_or(i == N_LAT_TILES - 1, i == CTX_TILE)
        xp = gq_ref[pl.ds(p0, 16), :].astype(F32)[15:16, :] * jnp.where(first_of_segment, 0.0, 1.0)
        xn = gq_ref[pl.ds(n0, 16), :].astype(F32)[0:1, :] * jnp.where(last_of_segment, 0.0, 1.0)
        row = lax.broadcasted_iota(jnp.int32, (TILE, GDN_QKV_WIDTH), 0)
        xdn = jnp.where(row == 0, xp, pltpu.roll(x, 1, 0))
        xup = jnp.where(row == TILE - 1, xn, pltpu.roll(x, TILE - 1, 0))
        c = _silu(cw_ref[0:1, :] * xdn + cw_ref[1:2, :] * x + cw_ref[2:3, :] * xup)
        q = c[:, 0:256]
        k = c[:, 256:512]
        q = q * lax.rsqrt(_seg_sum(q * q, seg) + NORM_EPS)
        k = k * lax.rsqrt(_seg_sum(k * k, seg) + NORM_EPS)
        q_s[pl.ds(r0, TILE), :] = q * (GDN_DK ** -0.5)
        k_s[pl.ds(r0, TILE), :] = k
        v_s[pl.ds(r0, TILE), :] = c[:, 512:768]
        bd = bd_ref[pl.ds(r0, TILE), :]
        b_s[pl.ds(r0, TILE), :] = _sigmoid(bd)
        g_s[pl.ds(r0, TILE), :] = -jnp.exp(alog_ref[...]) * _softplus(bd + dtb_ref[...])
        return carry

    lax.fori_loop(0, N_TILES, prep_tile, 0)

    G = GDN_GROUP
    R = G * C
    ri = lax.broadcasted_iota(jnp.int32, (C, GDN_WIDTH), 0)
    cj = lax.broadcasted_iota(jnp.int32, (C, GDN_WIDTH), 1) % C
    lane = lax.broadcasted_iota(jnp.int32, (R, LANES), 1)
    eye4 = jnp.where(ri == cj, 1.0, 0.0)
    same16 = (ri // 16) == (cj // 16)
    off32 = jnp.logical_and((ri // 32) == (cj // 32), jnp.logical_not(same16))
    off64 = (ri // 32) != (cj // 32)
    incl = (ri >= cj, ri <= cj)
    strict = (ri > cj, ri < cj)

    def block_diag(z):
        zb = z.astype(BF16)
        return jnp.concatenate([zb, zb, zb, zb], axis=0) * seg

    def phase1(it, carry):
        c0 = it * G
        r0 = pl.multiple_of(c0 * C, R)
        kc = k_s[pl.ds(r0, R), :]
        qc = q_s[pl.ds(r0, R), :]
        vc = v_s[pl.ds(r0, R), :]
        gt = g_s[pl.ds(r0, R), :]
        bt = b_s[pl.ds(r0, R), :]
        g_fwd = _tri_dot(tril_ref[...], gt)
        g_bwd = _tri_dot(triu_ref[...], gt)
        gc = jnp.where(lane < 8 + GDN_HEADS, g_fwd, g_bwd)
        gx = _dot_split(gc, eg_ref[...], 3)
        bx = _dot_split(bt, eb_ref[...], 2)
        gtx = jnp.concatenate([
            jnp.concatenate([jnp.broadcast_to(gx[g * C + C - 1:g * C + C, 0:GDN_WIDTH], (C, GDN_WIDTH)),
                             jnp.broadcast_to(gx[g * C:g * C + 1, GDN_WIDTH:], (C, GDN_WIDTH))], axis=1)
            for g in range(G)], axis=0)
        e_g = jnp.exp(gx)
        e_gk = jnp.exp(gtx - gx)
        eye_rows = jnp.concatenate([jnp.concatenate([eye4, eye4], axis=1)] * G, axis=0)
        grow = _tri_dot(ones_ref[...], jnp.where(eye_rows > 0.0, gx, 0.0))
        kb = kc.astype(BF16)
        qb = qc.astype(BF16)
        kk, qk0 = [], []
        for g in range(G):
            rs = slice(g * C, (g + 1) * C)
            bdk = jnp.concatenate([kb[rs]] * 4, axis=0) * seg
            kk.append(_dot_nt(kb[rs], bdk))
            qk0.append(_dot_nt(qb[rs], bdk))
        probs = [(g, d) for g in range(G) for d in range(2)]

        def part(arr, g, d):
            return arr[g * C:(g + 1) * C, d * GDN_WIDTH:(d + 1) * GDN_WIDTH]

        decay = [jnp.where(incl[d], jnp.exp(part(gx, g, d) - part(grow, g, d)), 0.0) for g, d in probs]
        a = [jnp.where(strict[d], part(bx, g, d) * kk[g] * dec, 0.0) for (g, d), dec in zip(probs, decay)]
        x = [jnp.where(same16, -ai, 0.0) for ai in a]
        inv = [eye4 + xi for xi in x]
        y = [_dot(xi.astype(BF16), block_diag(xi)) for xi in x]
        for stage in range(3):
            if stage < 2:
                prod = [_dot(yi.astype(BF16), jnp.concatenate([block_diag(ii), block_diag(yi)], axis=1))
                        for yi, ii in zip(y, inv)]
                inv = [ii + pi[:, 0:GDN_WIDTH] for ii, pi in zip(inv, prod)]
                y = [pi[:, GDN_WIDTH:] for pi in prod]
            else:
                inv = [ii + _dot(yi.astype(BF16), block_diag(ii)) for yi, ii in zip(y, inv)]
        for off in (off32, off64):
            t = [_dot(ii.astype(BF16), block_diag(jnp.where(off, ai, 0.0))) for ii, ai in zip(inv, a)]
            inv = [ii - _dot(ti.astype(BF16), block_diag(ii)) for ii, ti in zip(inv, t)]
        for (g, d), ii, dec in zip(probs, inv, decay):
            rs = slice(g * C, (g + 1) * C)
            bxi = part(bx, g, d)
            egi = part(e_g, g, d)
            rhs = jnp.concatenate([block_diag(vc[rs] * bxi), block_diag(kc[rs] * (bxi * egi))], axis=1)
            r = _dot(ii.astype(BF16), rhs)
            u_s[c0 + g, d] = r[:, 0:GDN_WIDTH]
            w_s[c0 + g, d] = r[:, GDN_WIDTH:].astype(BF16)
            qk_s[c0 + g, d] = (qk0[g] * dec).astype(BF16)
            qd_s[c0 + g, d] = (qc[rs] * egi).astype(BF16)
            kd_s[c0 + g, d] = (kc[rs] * part(e_gk, g, d)).astype(BF16)
            ge_s[c0 + g, d] = jnp.exp(part(gtx, g, d)[0:8, :])
        return carry

    lax.fori_loop(0, N_CHUNKS // G, phase1, 0)

    st_s[...] = jnp.zeros(st_s.shape, F32)
    o_s[...] = jnp.zeros(o_s.shape, F32)
    bi = lax.broadcasted_iota(jnp.int32, (GDN_WIDTH, GDN_WIDTH), 0) // C
    bj = lax.broadcasted_iota(jnp.int32, (GDN_WIDTH, GDN_WIDTH), 1) // C
    on_diag = bi == bj

    def phase2(s, carry):
        cs = (jnp.where(s < N_CTX_CHUNKS, s + N_LAT_CHUNKS, s - N_CTX_CHUNKS), N_CHUNKS - 1 - s)
        st = [st_s[d] for d in range(2)]
        stb = [x.astype(BF16) for x in st]
        r1 = [_dot(jnp.concatenate([w_s[cs[d], d], qd_s[cs[d], d]], axis=0), stb[d]) for d in range(2)]
        vn = [u_s[cs[d], d] - r1[d][0:C, :] for d in range(2)]
        vnb = [x.astype(BF16) for x in vn]
        o = [r1[d][C:2 * C, :] + _dot(qk_s[cs[d], d], jnp.concatenate([vnb[d]] * 4, axis=0) * seg)
             for d in range(2)]
        upd = [_dot_tn(kd_s[cs[d], d], vnb[d]) for d in range(2)]
        for d in range(2):
            st_s[d] = st[d] * ge_s[cs[d], d][0:1, :] + jnp.where(on_diag, upd[d], 0.0)
            r0 = pl.multiple_of(cs[d] * C, C)
            o_s[pl.ds(r0, C), :] += o[d]
        return carry

    lax.fori_loop(0, N_CHUNKS, phase2, 0)

    def finish_tile(i, carry):
        r0 = pl.multiple_of(i * TILE, TILE)
        o = o_s[pl.ds(r0, TILE), :]
        ms = _seg_sum(o * o, seg) * (1.0 / GDN_DK)
        y = o * lax.rsqrt(ms + NORM_EPS) * ng_ref[...]
        o_ref[pl.ds(r0, TILE), :] = (y * _silu(gg_ref[pl.ds(r0, TILE), :].astype(F32))).astype(BF16)
        return carry

    lax.fori_loop(0, N_TILES, finish_tile, 0)


def _gdn_call(gqkv, ggate, bd, conv_w, alog_row, dtb_row, ng_row, seg, tril, triu, ones, expand_g, expand_b):
    def per_batch(width):
        return pl.BlockSpec((None, STREAM, width), lambda b: (b, 0, 0), pipeline_mode=pl.Buffered(1))
    rows = GDN_GROUP * GDN_CHUNK
    packed = (N_CHUNKS, 2, GDN_CHUNK, GDN_WIDTH)
    return pl.pallas_call(
        _gdn_kernel,
        grid=(BATCH,),
        in_specs=[
            per_batch(768), per_batch(256), per_batch(LANES),
            _const_spec((3, 768)), _const_spec((1, LANES)), _const_spec((1, LANES)), _const_spec((1, GDN_WIDTH)),
            _const_spec((GDN_WIDTH, GDN_WIDTH)), _const_spec((rows, rows)), _const_spec((rows, rows)),
            _const_spec((rows, rows)), _const_spec((LANES, 2 * GDN_WIDTH)), _const_spec((LANES, 2 * GDN_WIDTH)),
        ],
        out_specs=pl.BlockSpec((None, STREAM, GDN_WIDTH), lambda b: (b, 0, 0)),
        out_shape=jax.ShapeDtypeStruct((BATCH, STREAM, GDN_WIDTH), BF16),
        scratch_shapes=[
            pltpu.VMEM((STREAM, GDN_WIDTH), F32),
            pltpu.VMEM((STREAM, GDN_WIDTH), F32),
            pltpu.VMEM((STREAM, GDN_WIDTH), F32),
            pltpu.VMEM((STREAM, LANES), F32),
            pltpu.VMEM((STREAM, LANES), F32),
            pltpu.VMEM(packed, F32),
            pltpu.VMEM(packed, BF16),
            pltpu.VMEM(packed, BF16),
            pltpu.VMEM(packed, BF16),
            pltpu.VMEM(packed, BF16),
            pltpu.VMEM((N_CHUNKS, 2, 8, GDN_WIDTH), F32),
            pltpu.VMEM((2, GDN_WIDTH, GDN_WIDTH), F32),
            pltpu.VMEM((STREAM, GDN_WIDTH), F32),
        ],
        compiler_params=_params("arbitrary"),
        name="gdn",
    )(gqkv, ggate, bd, conv_w, alog_row, dtb_row, ng_row, seg, tril, triu, ones, expand_g, expand_b)


def _hyena_filter_kernel(feat_ref, w1_ref, b1_ref, w2_ref, b2_ref, w3_ref, fr_ref, win_ref, fc_ref,
                         p_ref, q_ref, r_ref, *, n):
    fr = fr_ref[...]
    hid = jnp.sin(fr * (_dot(feat_ref[...].astype(BF16), w1_ref[...]) + b1_ref[...]))
    hid = jnp.sin(fr * (_dot(hid.astype(BF16), w2_ref[...]) + b2_ref[...]))
    filt = _dot(hid.astype(BF16), w3_ref[...])
    win = win_ref[...]
    row = lax.broadcasted_iota(jnp.int32, (n, HYENA_WIDTH), 0)
    h_fwd = filt[:, 0:HYENA_WIDTH] * win
    h_bwd = jnp.where(row == 0, 0.0, filt[:, HYENA_WIDTH:] * win)
    hcat = jnp.concatenate([h_fwd, h_bwd], axis=1).astype(BF16)
    h_re = _dot(fc_ref[:, 0:n], hcat)
    h_im = _dot(fc_ref[:, n:2 * n], hcat)
    sgn = jnp.where((row & 1) == 0, 1.0, -1.0)
    nyq = jnp.sum(sgn * (h_fwd + h_bwd), axis=0, keepdims=True)
    re = h_re[:, 0:HYENA_WIDTH] + h_re[:, HYENA_WIDTH:]
    p_ref[...] = re
    q_ref[...] = h_im[:, 0:HYENA_WIDTH] - h_im[:, HYENA_WIDTH:]
    r_ref[...] = jnp.where(row == 0, nyq, re)


def _hyena_filter_call(n, feats, w1, b1, w2, b2, w3, fr, win, fcat):
    shapes = [(n, LANES), (LANES, HYENA_FILTER_HIDDEN), (1, HYENA_FILTER_HIDDEN),
              (HYENA_FILTER_HIDDEN, HYENA_FILTER_HIDDEN), (1, HYENA_FILTER_HIDDEN),
              (HYENA_FILTER_HIDDEN, 2 * HYENA_WIDTH), (1, HYENA_FILTER_HIDDEN), (n, HYENA_WIDTH), (n, 2 * n)]
    out = jax.ShapeDtypeStruct((n, HYENA_WIDTH), F32)
    return pl.pallas_call(
        functools.partial(_hyena_filter_kernel, n=n),
        grid=(1,),
        in_specs=[_const_spec(s) for s in shapes],
        out_specs=[pl.BlockSpec((n, HYENA_WIDTH), lambda i: (0, 0))] * 3,
        out_shape=[out, out, out],
        compiler_params=_params("arbitrary"),
        name=f"hyena_filter_{n}",
    )(feats, w1, b1, w2, b2, w3, fr, win, fcat)


def _hyena_kernel(x_ref, cw_ref, bias_ref, fc_ref, p_ref, q_ref, r_ref, o_ref, w_s, x0_s, z_s, *, n):
    rows = min(n, TILE)
    n_tiles = n // rows
    inv_n = 1.0 / (2 * n)

    def conv_tile(i, carry):
        r0 = pl.multiple_of(i * rows, rows)
        x = x_ref[pl.ds(r0, rows), :].astype(F32)
        p0 = pl.multiple_of(jnp.maximum(r0 - 16, 0), 16)
        n0 = pl.multiple_of(jnp.minimum(r0 + rows, n - 16), 16)
        xp = x_ref[pl.ds(p0, 16), :].astype(F32)[15:16, :] * jnp.where(i == 0, 0.0, 1.0)
        xn = x_ref[pl.ds(n0, 16), :].astype(F32)[0:1, :] * jnp.where(i == n_tiles - 1, 0.0, 1.0)
        row = lax.broadcasted_iota(jnp.int32, (rows, 3 * HYENA_WIDTH), 0)
        xdn = jnp.where(row == 0, xp, pltpu.roll(x, 1, 0))
        xup = jnp.where(row == rows - 1, xn, pltpu.roll(x, rows - 1, 0))
        c = cw_ref[0:1, :] * xdn + cw_ref[1:2, :] * x + cw_ref[2:3, :] * xup
        x0_s[pl.ds(r0, rows), :] = c[:, 0:HYENA_WIDTH]
        w_s[pl.ds(r0, rows), :] = c[:, 2 * HYENA_WIDTH:] * c[:, HYENA_WIDTH:2 * HYENA_WIDTH]
        return carry

    lax.fori_loop(0, n_tiles, conv_tile, 0)

    w = w_s[...]
    wb = w.astype(BF16)
    u_re = _dot(fc_ref[:, 0:n], wb)
    u_im = _dot(fc_ref[:, n:2 * n], wb)
    row = lax.broadcasted_iota(jnp.int32, (n, HYENA_WIDTH), 0)
    sgn = jnp.where((row & 1) == 0, 1.0, -1.0)
    u_nyq = jnp.sum(sgn * w, axis=0, keepdims=True)
    u_im = jnp.where(row == 0, u_nyq, u_im)
    fscale = jnp.where(row == 0, inv_n, 2.0 * inv_n)
    z_re = u_re * p_ref[...] - u_im * q_ref[...]
    z_im = u_re * q_ref[...] + u_im * r_ref[...]
    z_s[0:n, :] = (z_re * fscale).astype(BF16)
    z_s[n:2 * n, :] = (z_im * fscale).astype(BF16)
    y = _dot(fc_ref[...], z_s[...]) + sgn * (z_im[0:1, :] * inv_n) + w * bias_ref[...]
    o_ref[...] = (y * x0_s[...]).astype(BF16)


def _hyena_call(n, tile_index, hy, conv_w, bias, fcat, hp, hq, hr):
    return pl.pallas_call(
        functools.partial(_hyena_kernel, n=n),
        grid=(BATCH,),
        in_specs=[
            pl.BlockSpec((None, n, 3 * HYENA_WIDTH), lambda b: (b, tile_index, 0)),
            _const_spec((3, 3 * HYENA_WIDTH)),
            _const_spec((1, HYENA_WIDTH)),
            _const_spec((n, 2 * n)),
            _const_spec((n, HYENA_WIDTH)), _const_spec((n, HYENA_WIDTH)), _const_spec((n, HYENA_WIDTH)),
        ],
        out_specs=pl.BlockSpec((None, n, HYENA_WIDTH), lambda b: (b, 0, 0)),
        out_shape=jax.ShapeDtypeStruct((BATCH, n, HYENA_WIDTH), BF16),
        scratch_shapes=[
            pltpu.VMEM((n, HYENA_WIDTH), F32),
            pltpu.VMEM((n, HYENA_WIDTH), F32),
            pltpu.VMEM((2 * n, HYENA_WIDTH), BF16),
        ],
        compiler_params=_params("arbitrary"),
        name=f"hyena_{n}",
    )(hy, conv_w, bias, fcat, hp, hq, hr)


def _out_kernel(h_ref, a_ref, g_ref, y_ref, m_ref, w_ref, o_ref):
    acc = _dot(a_ref[...], w_ref[0:512, :])
    acc = acc + _dot(g_ref[...], w_ref[512:768, :])
    acc = acc + _dot(y_ref[...], w_ref[768:1024, :])
    o_ref[...] = h_ref[...] + m_ref[2:3, :] * acc


def _out_call(h, attn, gdn, hy, mod, w_out, *, context):
    nt = 1 if context else N_LAT_TILES
    first = CTX_TILE if context else 0
    mod_row = (lambda b: CTX_MOD_ROW) if context else (lambda b: b)

    def tile(width, offset):
        return pl.BlockSpec((None, TILE, width), lambda b, t: (b, offset + t, 0))
    return pl.pallas_call(
        _out_kernel,
        grid=(BATCH, nt),
        in_specs=[
            tile(D_MODEL, 0), tile(ATTN_WIDTH, first), tile(GDN_WIDTH, first), tile(HYENA_WIDTH, 0),
            pl.BlockSpec((None, 6, D_MODEL), lambda b, t: (mod_row(b), 0, 0)),
            _const_spec((D_MODEL, D_MODEL)),
        ],
        out_specs=tile(D_MODEL, 0),
        out_shape=jax.ShapeDtypeStruct(h.shape, F32),
        input_output_aliases={0: 0},
        compiler_params=_params("arbitrary", "arbitrary"),
        name="out_proj_ctx" if context else "out_proj",
    )(h, attn, gdn, hy, mod, w_out)


def _ffn_kernel(h_ref, hp_ref, hn_ref, m_ref, g_ref, wu_ref, cw_ref, wd_ref, fg_ref, o_ref, act_s, *, rows, final):
    t = pl.program_id(1)
    ext = rows + 2 * HALO
    top = jnp.where(t == 0, 0.0, 1.0)
    bottom = jnp.where(t == pl.num_programs(1) - 1, 0.0, 1.0)
    gain, shift, scale = g_ref[...], m_ref[3:4, :], m_ref[4:5, :]
    u = jnp.concatenate([
        (_norm_mod(hp_ref[...], gain, shift, scale) * top).astype(BF16),
        _norm_mod(h_ref[...], gain, shift, scale).astype(BF16),
        (_norm_mod(hn_ref[...], gain, shift, scale) * bottom).astype(BF16)], axis=0)
    for j in range(N_FF_CHUNKS):
        p = _dot(u, wu_ref[j])
        cw = cw_ref[j]
        c = (cw[0:1, :] * pltpu.roll(p, 1, 0) + cw[1:2, :] * p + cw[2:3, :] * pltpu.roll(p, ext - 1, 0))
        c = c[HALO:HALO + rows, :]
        act_s[:, j * FF_CHUNK:(j + 1) * FF_CHUNK] = (_silu(c[:, 0:FF_CHUNK]) * c[:, FF_CHUNK:]).astype(BF16)
    res = h_ref[...] + m_ref[5:6, :] * _dot(act_s[...], wd_ref[...])
    if final:
        ms = jnp.mean(res * res, axis=-1, keepdims=True)
        res = res * lax.rsqrt(ms + NORM_EPS) * fg_ref[...]
    o_ref[...] = res


def _ffn_call(h, mod, gain, w_up, conv_w, w_down, final_gain, *, rows, context, final):
    seg_rows = h.shape[1]
    per = rows // HALO
    last = seg_rows // HALO - 1
    mod_row = (lambda b: CTX_MOD_ROW) if context else (lambda b: b)
    return pl.pallas_call(
        functools.partial(_ffn_kernel, rows=rows, final=final),
        grid=(BATCH, seg_rows // rows),
        in_specs=[
            pl.BlockSpec((None, rows, D_MODEL), lambda b, t: (b, t, 0)),
            pl.BlockSpec((None, HALO, D_MODEL), lambda b, t: (b, jnp.maximum(t * per - 1, 0), 0)),
            pl.BlockSpec((None, HALO, D_MODEL), lambda b, t: (b, jnp.minimum((t + 1) * per, last), 0)),
            pl.BlockSpec((None, 6, D_MODEL), lambda b, t: (mod_row(b), 0, 0)),
            _const_spec((1, D_MODEL)),
            _const_spec((N_FF_CHUNKS, D_MODEL, 2 * FF_CHUNK)),
            _const_spec((N_FF_CHUNKS, 8, 2 * FF_CHUNK)),
            _const_spec((D_FF, D_MODEL)),
            _const_spec((1, D_MODEL)),
        ],
        out_specs=pl.BlockSpec((None, rows, D_MODEL), lambda b, t: (b, t, 0)),
        out_shape=jax.ShapeDtypeStruct(h.shape, F32),
        scratch_shapes=[pltpu.VMEM((rows, D_FF), BF16)],
        compiler_params=_params("arbitrary", "arbitrary"),
        name="conv_ffn_ctx" if context else "conv_ffn",
    )(h, h, h, mod, gain, w_up, conv_w, w_down, final_gain)


def _block_diag_ones(width, block):
    i = jnp.arange(width)
    return (i[:, None] // block == i[None, :] // block).astype(BF16)


def _rope_tables():
    n = jnp.arange(SEQ)
    lane = jnp.arange(LANES)
    d = lane % HEAD_DIM
    comp = jnp.where((d < HEAD_DIM // 2)[None, :], (n // GRID_W)[:, None], (n % GRID_W)[:, None]).astype(F32)
    inv = ROPE_THETA ** (-(d % 16).astype(F32) / 16.0)
    ang = comp * inv[None, :]
    sign = jnp.where((d % 32) < 16, -1.0, 1.0)[None, :]
    cos_t = jnp.concatenate([jnp.cos(ang), jnp.ones((CTX_LEN, LANES), F32)], axis=0)
    sin_t = jnp.concatenate([jnp.sin(ang) * sign, jnp.zeros((CTX_LEN, LANES), F32)], axis=0)
    return cos_t, sin_t


def _dft_table(n):
    f = jnp.arange(n, dtype=jnp.int32)
    m = (f[:, None] * f[None, :]) % (2 * n)
    ang = m.astype(F32) * (math.pi / n)
    return jnp.concatenate([jnp.cos(ang), -jnp.sin(ang)], axis=1).astype(BF16)


def _hyena_static(n):
    t = jnp.linspace(0.0, 1.0, n, dtype=F32)[:, None]
    omega = 2.0 * math.pi * jnp.arange(n, dtype=F32)[:, None] / n
    bands = jnp.linspace(1e-4, HYENA_EMB_BANDS - 1, HYENA_EMB_BANDS, dtype=F32)[None, :]
    feats = jnp.concatenate([t, jnp.cos(bands * omega), -jnp.sin(bands * omega)], axis=-1)
    feats = jnp.pad(feats, ((0, 0), (0, LANES - feats.shape[1])))
    max_decay = math.log(HYENA_DECAY_TARGET) / HYENA_FAST_DECAY
    min_decay = math.log(HYENA_DECAY_TARGET) / HYENA_SLOW_DECAY
    deltas = jnp.abs(jnp.linspace(min_decay, max_decay, HYENA_WIDTH, dtype=F32))
    window = jnp.exp(-t * deltas[None, :])
    return feats, window


def kernel(x, c, ctx, c_ctx, w_ada, b_ada, norm1, norm2, w_in, w_out, q_gain, k_gain, gdn_conv, gdn_a_log,
           gdn_dt_bias, gdn_norm, hyena_conv, hyena_w1, hyena_b1, hyena_w2, hyena_b2, hyena_w3, hyena_freq,
           hyena_bias, ffn_up, ffn_conv, ffn_down, final_norm):
    assert x.shape == (BATCH, SEQ, D_MODEL) and ctx.shape == (BATCH, CTX_LEN, D_MODEL)

    h_lat, h_ctx = x, ctx
    cvec = jnp.concatenate([c, c_ctx[None, :], jnp.zeros((MOD_ROWS - BATCH - 1, D_MODEL), F32)], axis=0)
    mod_all = _ada_call(cvec, w_ada, b_ada).reshape(DEPTH, MOD_ROWS, 6, D_MODEL)

    cos_t, sin_t = _rope_tables()
    seg_head = _block_diag_ones(LANES, HEAD_DIM)
    seg_gdn = _block_diag_ones(GDN_WIDTH, GDN_DK)
    ii = jnp.arange(GDN_GROUP * GDN_CHUNK)
    same_chunk = (ii[:, None] // GDN_CHUNK) == (ii[None, :] // GDN_CHUNK)
    tril = jnp.logical_and(same_chunk, ii[:, None] >= ii[None, :]).astype(BF16)
    triu = jnp.logical_and(same_chunk, ii[:, None] <= ii[None, :]).astype(BF16)
    chunk_ones = same_chunk.astype(BF16)
    col = jnp.arange(2 * GDN_WIDTH) // GDN_DK
    expand_g = (jnp.arange(LANES)[:, None] == 8 + col[None, :]).astype(BF16)
    expand_b = (jnp.arange(LANES)[:, None] == col[None, :]).astype(BF16)
    fcat_lat = _dft_table(SEQ)
    fcat_ctx = _dft_table(CTX_LEN)
    feats_lat, win_lat = _hyena_static(SEQ)
    feats_ctx, win_ctx = _hyena_static(CTX_LEN)

    for i in range(DEPTH):
        with_ctx = i < DEPTH - 1
        mod = mod_all[i]
        wi = w_in[i]
        w_cat = jnp.concatenate(
            [wi[:, 0:768], wi[:, 768:1536], wi[:, 1536:1792], wi[:, 1808:2576], wi[:, 1792:1808],
             jnp.zeros((D_MODEL, LANES - 16), F32)], axis=1).astype(BF16)
        aqkv, gqkv, ggate, hyp, bd = _in_call(h_lat, h_ctx, mod, norm1[i][None, :], w_cat)

        q_pad, kt, v = _attn_prep_call(aqkv, cos_t, sin_t, jnp.tile(q_gain[i], 2)[None, :],
                                       jnp.tile(k_gain[i], 2)[None, :], seg_head)
        attn = _attn_call(q_pad, kt, v, with_ctx)

        pad8 = jnp.zeros((8,), F32)
        alog_row = jnp.concatenate([pad8, gdn_a_log[i].reshape(-1), jnp.zeros((LANES - 16,), F32)])[None, :]
        dtb_row = jnp.concatenate([pad8, gdn_dt_bias[i].reshape(-1), jnp.zeros((LANES - 16,), F32)])[None, :]
        gdn = _gdn_call(gqkv, ggate, bd, gdn_conv[i], alog_row, dtb_row, jnp.tile(gdn_norm[i], GDN_HEADS)[None, :],
                        seg_gdn, tril, triu, chunk_ones, expand_g, expand_b)

        w1 = jnp.pad(hyena_w1[i], ((0, LANES - hyena_w1.shape[1]), (0, 0))).astype(BF16)
        filt_args = (w1, hyena_b1[i][None, :], hyena_w2[i].astype(BF16), hyena_b2[i][None, :],
                     hyena_w3[i].astype(BF16), hyena_freq[i][None, :])
        hp, hq, hr = _hyena_filter_call(SEQ, feats_lat, *filt_args, win_lat, fcat_lat)
        hy_lat = _hyena_call(SEQ, 0, hyp, hyena_conv[i], hyena_bias[i], fcat_lat, hp, hq, hr)
        wo = w_out[i].astype(BF16)
        h_lat = _out_call(h_lat, attn, gdn, hy_lat, mod, wo, context=False)
        if with_ctx:
            hp, hq, hr = _hyena_filter_call(CTX_LEN, feats_ctx, *filt_args, win_ctx, fcat_ctx)
            hy_ctx = _hyena_call(CTX_LEN, CTX_TILE, hyp, hyena_conv[i], hyena_bias[i], fcat_ctx, hp, hq, hr)
            h_ctx = _out_call(h_ctx, attn, gdn, hy_ctx, mod, wo, context=True)

        wu = ffn_up[i]
        w_up = jnp.concatenate([wu[:, 0:D_FF].reshape(D_MODEL, N_FF_CHUNKS, FF_CHUNK),
                                wu[:, D_FF:].reshape(D_MODEL, N_FF_CHUNKS, FF_CHUNK)], axis=2)
        w_up = jnp.transpose(w_up, (1, 0, 2)).astype(BF16)
        fc = ffn_conv[i]
        conv_w = jnp.concatenate([fc[:, 0:D_FF].reshape(3, N_FF_CHUNKS, FF_CHUNK),
                                  fc[:, D_FF:].reshape(3, N_FF_CHUNKS, FF_CHUNK)], axis=2)
        conv_w = jnp.pad(jnp.transpose(conv_w, (1, 0, 2)), ((0, 0), (0, 5), (0, 0)))
        ffn_args = (mod, norm2[i][None, :], w_up, conv_w, ffn_down[i].astype(BF16), final_norm[None, :])
        h_lat = _ffn_call(h_lat, *ffn_args, rows=FFN_ROWS, context=False, final=not with_ctx)
        if with_ctx:
            h_ctx = _ffn_call(h_ctx, *ffn_args, rows=CTX_LEN, context=True, final=False)

    return h_lat
```

```python
import functools
import math

import jax
import jax.numpy as jnp
from jax import lax
from jax.experimental import pallas as pl
from jax.experimental.pallas import tpu as pltpu

F32 = jnp.float32
BF16 = jnp.bfloat16

D_MODEL = 1024
BATCH = 16
SEQ = 2048
DEPTH = 4
CTX_LEN = 256
STREAM = SEQ + CTX_LEN
TILE = 256
N_TILES = STREAM // TILE
N_LAT_TILES = SEQ // TILE
CTX_TILE = N_LAT_TILES
GRID_W = 64
NORM_EPS = 1e-6
LOG2_E = math.log2(math.e)
HEAD_DIM = 64
LANES = 128
MOD_ROWS = 24
CTX_MOD_ROW = BATCH

ATTN_WIDTH = 512
ATTN_HEADS = 8
ATTN_KV_WIDTH = 128
ROPE_THETA = 10000.0

GDN_WIDTH = 256
GDN_HEADS = 4
GDN_DK = 64
GDN_QKV_WIDTH = 768
GDN_CHUNK = 64
N_CHUNKS = STREAM // GDN_CHUNK
N_LAT_CHUNKS = SEQ // GDN_CHUNK
N_CTX_CHUNKS = CTX_LEN // GDN_CHUNK
GDN_GROUP = 4

HYENA_WIDTH = 256
HYENA_EMB_BANDS = 16
HYENA_FILTER_HIDDEN = 64
HYENA_FAST_DECAY = 0.3
HYENA_SLOW_DECAY = 1.5
HYENA_DECAY_TARGET = 1e-2

D_FF = 2816
FF_CHUNK = 256
N_FF_CHUNKS = D_FF // FF_CHUNK
HALO = 8
FFN_ROWS = 512

COL_ATTN = (0, 768)
COL_GQKV = (768, 1536)
COL_GGATE = (1536, 1792)
COL_HY = (1792, 2560)
COL_BD = (2560, 2688)
IN_COLS = 2688

VMEM_LIMIT = 56 * 1024 * 1024


def _dot(a, b):
    return jnp.dot(a, b, preferred_element_type=F32)


def _dot_nt(a, b):
    return lax.dot_general(a, b, (((1,), (1,)), ((), ())), preferred_element_type=F32)


def _dot_tn(a, b):
    return lax.dot_general(a, b, (((0,), (0,)), ((), ())), preferred_element_type=F32)


def _sigmoid(x):
    return 1.0 / (1.0 + jnp.exp(-x))


def _silu(x):
    return x * _sigmoid(x)


def _softplus(x):
    return jnp.maximum(x, 0.0) + jnp.log(1.0 + jnp.exp(-jnp.abs(x)))


def _seg_sum(x, seg):
    hi = x.astype(BF16)
    lo = (x - hi.astype(F32)).astype(BF16)
    return _dot(hi, seg) + _dot(lo, seg)


def _tri_dot(tri, x):
    x1 = x.astype(BF16)
    r1 = x - x1.astype(F32)
    x2 = r1.astype(BF16)
    x3 = (r1 - x2.astype(F32)).astype(BF16)
    return _dot(tri, x1) + _dot(tri, x2) + _dot(tri, x3)


def _dot_split(x, sel, terms):
    out = None
    for _ in range(terms):
        piece = x.astype(BF16)
        x = x - piece.astype(F32)
        out = _dot(piece, sel) if out is None else out + _dot(piece, sel)
    return out


def _params(*sem):
    return pltpu.CompilerParams(dimension_semantics=sem, vmem_limit_bytes=VMEM_LIMIT)


def _const_spec(shape):
    nd = len(shape)
    return pl.BlockSpec(shape, lambda *_: (0,) * nd, pipeline_mode=pl.Buffered(1))


def _mod_row(b, t):
    return jnp.where(t == CTX_TILE, CTX_MOD_ROW, b)


def _ada_kernel(c_ref, w_ref, b_ref, o_ref):
    s = _silu(c_ref[...]).astype(BF16)
    o_ref[...] = _dot(s, w_ref[...].astype(BF16)) + b_ref[...]


def _ada_call(cvec, w_ada, b_ada):
    nj = 6
    return pl.pallas_call(
        _ada_kernel,
        grid=(DEPTH, nj),
        in_specs=[
            pl.BlockSpec((MOD_ROWS, D_MODEL), lambda i, j: (0, 0)),
            pl.BlockSpec((None, D_MODEL, D_MODEL), lambda i, j: (i, 0, j)),
            pl.BlockSpec((None, 1, D_MODEL), lambda i, j: (i, 0, j)),
        ],
        out_specs=pl.BlockSpec((None, MOD_ROWS, D_MODEL), lambda i, j: (i, 0, j)),
        out_shape=jax.ShapeDtypeStruct((DEPTH, MOD_ROWS, 6 * D_MODEL), F32),
        compiler_params=_params("arbitrary", "arbitrary"),
        name="adaln",
    )(cvec, w_ada, b_ada.reshape(DEPTH, 1, 6 * D_MODEL))


def _norm_mod(x, gain, shift, scale):
    ms = jnp.mean(x * x, axis=-1, keepdims=True)
    return (x * lax.rsqrt(ms + NORM_EPS) * gain) * (1.0 + scale) + shift


def _in_kernel(xl_ref, xc_ref, m_ref, g_ref, w_ref, cos_ref, sin_ref, qg_ref, kg_ref, seg_ref,
               q_ref, kt_ref, v_ref, ogq_ref, ogg_ref, ohy_ref, obd_ref):
    x = jnp.where(pl.program_id(1) == CTX_TILE, xc_ref[...], xl_ref[...])
    u = _norm_mod(x, g_ref[...], m_ref[0:1, :], m_ref[1:2, :]).astype(BF16)
    ogq_ref[...] = _dot(u, w_ref[:, COL_GQKV[0]:COL_GQKV[1]]).astype(BF16)
    ogg_ref[...] = _dot(u, w_ref[:, COL_GGATE[0]:COL_GGATE[1]]).astype(BF16)
    ohy_ref[...] = _dot(u, w_ref[:, COL_HY[0]:COL_HY[1]]).astype(BF16)
    obd_ref[...] = _dot(u, w_ref[:, COL_BD[0]:COL_BD[1]])

    a = _dot(u, w_ref[:, COL_ATTN[0]:COL_ATTN[1]])
    lane = lax.broadcasted_iota(jnp.int32, (TILE, LANES), 1)
    first = (lane % 32) < 16
    low = lane < HEAD_DIM
    seg = seg_ref[...]
    cosv = cos_ref[...]
    sinv = sin_ref[...]

    def norm_rope(x, gain):
        ms = _seg_sum(x * x, seg) * (1.0 / HEAD_DIM)
        y = x * lax.rsqrt(ms + NORM_EPS) * gain
        partner = jnp.where(first, pltpu.roll(y, LANES - 16, 1), pltpu.roll(y, 16, 1))
        return y * cosv + partner * sinv

    for j in range(ATTN_HEADS // 2):
        r = norm_rope(a[:, j * LANES:(j + 1) * LANES], qg_ref[...]) * (HEAD_DIM ** -0.5 * LOG2_E)
        swapped = pltpu.roll(r, HEAD_DIM, 1)
        if j // 2 == 0:
            qa = jnp.where(low, r, 0.0)
            qb = jnp.where(low, swapped, 0.0)
        else:
            qa = jnp.where(low, 0.0, swapped)
            qb = jnp.where(low, 0.0, r)
        q_ref[:, (2 * j) * LANES:(2 * j + 1) * LANES] = qa.astype(BF16)
        q_ref[:, (2 * j + 1) * LANES:(2 * j + 2) * LANES] = qb.astype(BF16)
    rk = norm_rope(a[:, ATTN_WIDTH:ATTN_WIDTH + ATTN_KV_WIDTH], kg_ref[...])
    kt_ref[...] = rk.T.astype(BF16)
    v_ref[...] = a[:, ATTN_WIDTH + ATTN_KV_WIDTH:].astype(BF16)


def _in_call(h_lat, h_ctx, mod, gain, w_cat, cos_t, sin_t, qg, kg, seg):
    def tile(width):
        return pl.BlockSpec((None, TILE, width), lambda b, t: (b, t, 0))
    return pl.pallas_call(
        _in_kernel,
        grid=(BATCH, N_TILES),
        in_specs=[
            pl.BlockSpec((None, TILE, D_MODEL), lambda b, t: (b, jnp.minimum(t, N_LAT_TILES - 1), 0)),
            pl.BlockSpec((None, TILE, D_MODEL), lambda b, t: (b, 0, 0)),
            pl.BlockSpec((None, 6, D_MODEL), lambda b, t: (_mod_row(b, t), 0, 0)),
            _const_spec((1, D_MODEL)),
            _const_spec((D_MODEL, IN_COLS)),
            pl.BlockSpec((TILE, LANES), lambda b, t: (t, 0)),
            pl.BlockSpec((TILE, LANES), lambda b, t: (t, 0)),
            _const_spec((1, LANES)),
            _const_spec((1, LANES)),
            _const_spec((LANES, LANES)),
        ],
        out_specs=[
            tile(ATTN_HEADS * LANES),
            pl.BlockSpec((None, LANES, TILE), lambda b, t: (b, 0, t)),
            tile(LANES), tile(768), tile(256), tile(768), tile(LANES)],
        out_shape=[
            jax.ShapeDtypeStruct((BATCH, STREAM, ATTN_HEADS * LANES), BF16),
            jax.ShapeDtypeStruct((BATCH, LANES, STREAM), BF16),
            jax.ShapeDtypeStruct((BATCH, STREAM, LANES), BF16),
            jax.ShapeDtypeStruct((BATCH, STREAM, 768), BF16),
            jax.ShapeDtypeStruct((BATCH, STREAM, 256), BF16),
            jax.ShapeDtypeStruct((BATCH, STREAM, 768), BF16),
            jax.ShapeDtypeStruct((BATCH, STREAM, LANES), F32),
        ],
        compiler_params=_params("arbitrary", "arbitrary"),
        name="in_proj",
    )(h_lat, h_ctx, mod, gain, w_cat, cos_t, sin_t, qg, kg, seg)


def _attn_kernel(q_ref, kt_ref, v_ref, o_ref, *, with_ctx):
    low = lax.broadcasted_iota(jnp.int32, (TILE, LANES), 1) < HEAD_DIM

    def run(k0, k1):
        kt = kt_ref[:, k0:k1]
        v = v_ref[k0:k1, :]
        s_next = _dot(q_ref[:, 0:LANES], kt)
        outs = []
        for hq in range(ATTN_HEADS):
            s = s_next
            if hq + 1 < ATTN_HEADS:
                s_next = _dot(q_ref[:, (hq + 1) * LANES:(hq + 2) * LANES], kt)
            p = jnp.exp2(s - jnp.max(s, axis=-1, keepdims=True))
            den = jnp.sum(p, axis=-1, keepdims=True)
            outs.append(_dot(p.astype(BF16), v) / den)
            if hq % 2 == 1:
                j = hq // 2
                if j // 2 == 0:
                    blk = jnp.where(low, outs[0], pltpu.roll(outs[1], HEAD_DIM, 1))
                else:
                    blk = jnp.where(low, pltpu.roll(outs[0], HEAD_DIM, 1), outs[1])
                o_ref[:, j * LANES:(j + 1) * LANES] = blk.astype(BF16)
                outs = []

    if with_ctx:
        t = pl.program_id(1)

        @pl.when(t < CTX_TILE)
        def _():
            run(0, STREAM)

        @pl.when(t == CTX_TILE)
        def _():
            run(SEQ, STREAM)
    else:
        run(0, STREAM)


def _attn_call(q_pad, kt, v, with_ctx):
    nt = N_TILES if with_ctx else N_LAT_TILES
    return pl.pallas_call(
        functools.partial(_attn_kernel, with_ctx=with_ctx),
        grid=(BATCH, nt),
        in_specs=[
            pl.BlockSpec((None, TILE, ATTN_HEADS * LANES), lambda b, t: (b, t, 0)),
            pl.BlockSpec((None, LANES, STREAM), lambda b, t: (b, 0, 0)),
            pl.BlockSpec((None, STREAM, LANES), lambda b, t: (b, 0, 0)),
        ],
        out_specs=pl.BlockSpec((None, TILE, ATTN_WIDTH), lambda b, t: (b, t, 0)),
        out_shape=jax.ShapeDtypeStruct((BATCH, STREAM, ATTN_WIDTH), BF16),
        compiler_params=_params("arbitrary", "arbitrary"),
        name="attention",
    )(q_pad, kt, v)


def _gdn_kernel(gq_ref, gg_ref, bd_ref, cw_ref, alog_ref, dtb_ref, ng_ref, seg_ref, tril_ref, triu_ref, ones_ref,
                eg_ref, eb_ref, o_ref, q_s, k_s, v_s, g_s, b_s, u_s, w_s, qk_s, qd_s, kd_s, ge_s, st_s, o_s):
    C = GDN_CHUNK
    seg = seg_ref[...]

    def prep_tile(i, carry):
        r0 = pl.multiple_of(i * TILE, TILE)
        x = gq_ref[pl.ds(r0, TILE), :].astype(F32)
        p0 = pl.multiple_of(jnp.maximum(r0 - 16, 0), 16)
        n0 = pl.multiple_of(jnp.minimum(r0 + TILE, STREAM - 16), 16)
        first_of_segment = jnp.logical_or(i == 0, i == CTX_TILE)
        last_of_segment = jnp.logical_or(i == N_LAT_TILES - 1, i == CTX_TILE)
        xp = gq_ref[pl.ds(p0, 16), :].astype(F32)[15:16, :] * jnp.where(first_of_segment, 0.0, 1.0)
        xn = gq_ref[pl.ds(n0, 16), :].astype(F32)[0:1, :] * jnp.where(last_of_segment, 0.0, 1.0)
        row = lax.broadcasted_iota(jnp.int32, (TILE, GDN_QKV_WIDTH), 0)
        xdn = jnp.where(row == 0, xp, pltpu.roll(x, 1, 0))
        xup = jnp.where(row == TILE - 1, xn, pltpu.roll(x, TILE - 1, 0))
        c = _silu(cw_ref[0:1, :] * xdn + cw_ref[1:2, :] * x + cw_ref[2:3, :] * xup)
        q = c[:, 0:256]
        k = c[:, 256:512]
        q = q * lax.rsqrt(_seg_sum(q * q, seg) + NORM_EPS)
        k = k * lax.rsqrt(_seg_sum(k * k, seg) + NORM_EPS)
        q_s[pl.ds(r0, TILE), :] = q * (GDN_DK ** -0.5)
        k_s[pl.ds(r0, TILE), :] = k
        v_s[pl.ds(r0, TILE), :] = c[:, 512:768]
        bd = bd_ref[pl.ds(r0, TILE), :]
        b_s[pl.ds(r0, TILE), :] = _sigmoid(bd)
        g_s[pl.ds(r0, TILE), :] = -jnp.exp(alog_ref[...]) * _softplus(bd + dtb_ref[...])
        return carry

    lax.fori_loop(0, N_TILES, prep_tile, 0)

    G = GDN_GROUP
    R = G * C
    ri = lax.broadcasted_iota(jnp.int32, (C, GDN_WIDTH), 0)
    cj = lax.broadcasted_iota(jnp.int32, (C, GDN_WIDTH), 1) % C
    lane = lax.broadcasted_iota(jnp.int32, (R, LANES), 1)
    eye4 = jnp.where(ri == cj, 1.0, 0.0)
    same16 = (ri // 16) == (cj // 16)
    off32 = jnp.logical_and((ri // 32) == (cj // 32), jnp.logical_not(same16))
    off64 = (ri // 32) != (cj // 32)
    incl = (ri >= cj, ri <= cj)
    strict = (ri > cj, ri < cj)

    def block_diag(z):
        zb = z.astype(BF16)
        return jnp.concatenate([zb, zb, zb, zb], axis=0) * seg

    def phase1(it, carry):
        c0 = it * G
        r0 = pl.multiple_of(c0 * C, R)
        kc = k_s[pl.ds(r0, R), :]
        qc = q_s[pl.ds(r0, R), :]
        vc = v_s[pl.ds(r0, R), :]
        gt = g_s[pl.ds(r0, R), :]
        bt = b_s[pl.ds(r0, R), :]
        g_fwd = _tri_dot(tril_ref[...], gt)
        g_bwd = _tri_dot(triu_ref[...], gt)
        gc = jnp.where(lane < 8 + GDN_HEADS, g_fwd, g_bwd)
        gx = _dot_split(gc, eg_ref[...], 3)
        bx = _dot_split(bt, eb_ref[...], 2)
        gtx = jnp.concatenate([
            jnp.concatenate([jnp.broadcast_to(gx[g * C + C - 1:g * C + C, 0:GDN_WIDTH], (C, GDN_WIDTH)),
                             jnp.broadcast_to(gx[g * C:g * C + 1, GDN_WIDTH:], (C, GDN_WIDTH))], axis=1)
            for g in range(G)], axis=0)
        e_g = jnp.exp(gx)
        e_gk = jnp.exp(gtx - gx)
        eye_rows = jnp.concatenate([jnp.concatenate([eye4, eye4], axis=1)] * G, axis=0)
        grow = _tri_dot(ones_ref[...], jnp.where(eye_rows > 0.0, gx, 0.0))
        kb = kc.astype(BF16)
        qb = qc.astype(BF16)
        kk, qk0 = [], []
        for g in range(G):
            rs = slice(g * C, (g + 1) * C)
            bdk = jnp.concatenate([kb[rs]] * 4, axis=0) * seg
            kk.append(_dot_nt(kb[rs], bdk))
            qk0.append(_dot_nt(qb[rs], bdk))
        probs = [(g, d) for g in range(G) for d in range(2)]

        def part(arr, g, d):
            return arr[g * C:(g + 1) * C, d * GDN_WIDTH:(d + 1) * GDN_WIDTH]

        decay = [jnp.where(incl[d], jnp.exp(part(gx, g, d) - part(grow, g, d)), 0.0) for g, d in probs]
        a = [jnp.where(strict[d], part(bx, g, d) * kk[g] * dec, 0.0) for (g, d), dec in zip(probs, decay)]
        x = [jnp.where(same16, -ai, 0.0) for ai in a]
        inv = [eye4 + xi for xi in x]
        y = [_dot(xi.astype(BF16), block_diag(xi)) for xi in x]
        for stage in range(3):
            if stage < 2:
                prod = [_dot(yi.astype(BF16), jnp.concatenate([block_diag(ii), block_diag(yi)], axis=1))
                        for yi, ii in zip(y, inv)]
                inv = [ii + pi[:, 0:GDN_WIDTH] for ii, pi in zip(inv, prod)]
                y = [pi[:, GDN_WIDTH:] for pi in prod]
            else:
                inv = [ii + _dot(yi.astype(BF16), block_diag(ii)) for yi, ii in zip(y, inv)]
        for off in (off32, off64):
            t = [_dot(ii.astype(BF16), block_diag(jnp.where(off, ai, 0.0))) for ii, ai in zip(inv, a)]
            inv = [ii - _dot(ti.astype(BF16), block_diag(ii)) for ii, ti in zip(inv, t)]
        for (g, d), ii, dec in zip(probs, inv, decay):
            rs = slice(g * C, (g + 1) * C)
            bxi = part(bx, g, d)
            egi = part(e_g, g, d)
            rhs = jnp.concatenate([block_diag(vc[rs] * bxi), block_diag(kc[rs] * (bxi * egi))], axis=1)
            r = _dot(ii.astype(BF16), rhs)
            u_s[c0 + g, d] = r[:, 0:GDN_WIDTH]
            w_s[c0 + g, d] = r[:, GDN_WIDTH:].astype(BF16)
            qk_s[c0 + g, d] = (qk0[g] * dec).astype(BF16)
            qd_s[c0 + g, d] = (qc[rs] * egi).astype(BF16)
            kd_s[c0 + g, d] = (kc[rs] * part(e_gk, g, d)).astype(BF16)
            ge_s[c0 + g, d] = jnp.exp(part(gtx, g, d)[0:8, :])
        return carry

    lax.fori_loop(0, N_CHUNKS // G, phase1, 0)

    st_s[...] = jnp.zeros(st_s.shape, F32)
    o_s[...] = jnp.zeros(o_s.shape, F32)
    bi = lax.broadcasted_iota(jnp.int32, (GDN_WIDTH, GDN_WIDTH), 0) // C
    bj = lax.broadcasted_iota(jnp.int32, (GDN_WIDTH, GDN_WIDTH), 1) // C
    on_diag = bi == bj

    def phase2(s, carry):
        cs = (jnp.where(s < N_CTX_CHUNKS, s + N_LAT_CHUNKS, s - N_CTX_CHUNKS), N_CHUNKS - 1 - s)
        st = [st_s[d] for d in range(2)]
        stb = [x.astype(BF16) for x in st]
        r1 = [_dot(jnp.concatenate([w_s[cs[d], d], qd_s[cs[d], d]], axis=0), stb[d]) for d in range(2)]
        vn = [u_s[cs[d], d] - r1[d][0:C, :] for d in range(2)]
        vnb = [x.astype(BF16) for x in vn]
        o = [r1[d][C:2 * C, :] + _dot(qk_s[cs[d], d], jnp.concatenate([vnb[d]] * 4, axis=0) * seg)
             for d in range(2)]
        upd = [_dot_tn(kd_s[cs[d], d], vnb[d]) for d in range(2)]
        for d in range(2):
            st_s[d] = st[d] * ge_s[cs[d], d][0:1, :] + jnp.where(on_diag, upd[d], 0.0)
            r0 = pl.multiple_of(cs[d] * C, C)
            o_s[pl.ds(r0, C), :] += o[d]
        return carry

    lax.fori_loop(0, N_CHUNKS, phase2, 0)

    def finish_tile(i, carry):
        r0 = pl.multiple_of(i * TILE, TILE)
        o = o_s[pl.ds(r0, TILE), :]
        ms = _seg_sum(o * o, seg) * (1.0 / GDN_DK)
        y = o * lax.rsqrt(ms + NORM_EPS) * ng_ref[...]
        o_ref[pl.ds(r0, TILE), :] = (y * _silu(gg_ref[pl.ds(r0, TILE), :].astype(F32))).astype(BF16)
        return carry

    lax.fori_loop(0, N_TILES, finish_tile, 0)


def _gdn_call(gqkv, ggate, bd, conv_w, alog_row, dtb_row, ng_row, seg, tril, triu, ones, expand_g, expand_b):
    def per_batch(width):
        return pl.BlockSpec((None, STREAM, width), lambda b: (b, 0, 0), pipeline_mode=pl.Buffered(1))
    rows = GDN_GROUP * GDN_CHUNK
    packed = (N_CHUNKS, 2, GDN_CHUNK, GDN_WIDTH)
    return pl.pallas_call(
        _gdn_kernel,
        grid=(BATCH,),
        in_specs=[
            per_batch(768), per_batch(256), per_batch(LANES),
            _const_spec((3, 768)), _const_spec((1, LANES)), _const_spec((1, LANES)), _const_spec((1, GDN_WIDTH)),
            _const_spec((GDN_WIDTH, GDN_WIDTH)), _const_spec((rows, rows)), _const_spec((rows, rows)),
            _const_spec((rows, rows)), _const_spec((LANES, 2 * GDN_WIDTH)), _const_spec((LANES, 2 * GDN_WIDTH)),
        ],
        out_specs=pl.BlockSpec((None, STREAM, GDN_WIDTH), lambda b: (b, 0, 0)),
        out_shape=jax.ShapeDtypeStruct((BATCH, STREAM, GDN_WIDTH), BF16),
        scratch_shapes=[
            pltpu.VMEM((STREAM, GDN_WIDTH), F32),
            pltpu.VMEM((STREAM, GDN_WIDTH), F32),
            pltpu.VMEM((STREAM, GDN_WIDTH), F32),
            pltpu.VMEM((STREAM, LANES), F32),
            pltpu.VMEM((STREAM, LANES), F32),
            pltpu.VMEM(packed, F32),
            pltpu.VMEM(packed, BF16),
            pltpu.VMEM(packed, BF16),
            pltpu.VMEM(packed, BF16),
            pltpu.VMEM(packed, BF16),
            pltpu.VMEM((N_CHUNKS, 2, 8, GDN_WIDTH), F32),
            pltpu.VMEM((2, GDN_WIDTH, GDN_WIDTH), F32),
            pltpu.VMEM((STREAM, GDN_WIDTH), F32),
        ],
        compiler_params=_params("arbitrary"),
        name="gdn",
    )(gqkv, ggate, bd, conv_w, alog_row, dtb_row, ng_row, seg, tril, triu, ones, expand_g, expand_b)


def _hyena_filter_kernel(feat_ref, w1_ref, b1_ref, w2_ref, b2_ref, w3_ref, fr_ref, win_ref, fc_ref,
                         p_ref, q_ref, r_ref, *, n):
    fr = fr_ref[...]
    hid = jnp.sin(fr * (_dot(feat_ref[...].astype(BF16), w1_ref[...]) + b1_ref[...]))
    hid = jnp.sin(fr * (_dot(hid.astype(BF16), w2_ref[...]) + b2_ref[...]))
    filt = _dot(hid.astype(BF16), w3_ref[...])
    win = win_ref[...]
    row = lax.broadcasted_iota(jnp.int32, (n, HYENA_WIDTH), 0)
    h_fwd = filt[:, 0:HYENA_WIDTH] * win
    h_bwd = jnp.where(row == 0, 0.0, filt[:, HYENA_WIDTH:] * win)
    hcat = jnp.concatenate([h_fwd, h_bwd], axis=1).astype(BF16)
    h_re = _dot(fc_ref[:, 0:n], hcat)
    h_im = _dot(fc_ref[:, n:2 * n], hcat)
    sgn = jnp.where((row & 1) == 0, 1.0, -1.0)
    nyq = jnp.sum(sgn * (h_fwd + h_bwd), axis=0, keepdims=True)
    re = h_re[:, 0:HYENA_WIDTH] + h_re[:, HYENA_WIDTH:]
    p_ref[...] = re
    q_ref[...] = h_im[:, 0:HYENA_WIDTH] - h_im[:, HYENA_WIDTH:]
    r_ref[...] = jnp.where(row == 0, nyq, re)


def _hyena_filter_call(n, feats, w1, b1, w2, b2, w3, fr, win, fcat):
    shapes = [(n, LANES), (LANES, HYENA_FILTER_HIDDEN), (1, HYENA_FILTER_HIDDEN),
              (HYENA_FILTER_HIDDEN, HYENA_FILTER_HIDDEN), (1, HYENA_FILTER_HIDDEN),
              (HYENA_FILTER_HIDDEN, 2 * HYENA_WIDTH), (1, HYENA_FILTER_HIDDEN), (n, HYENA_WIDTH), (n, 2 * n)]
    out = jax.ShapeDtypeStruct((n, HYENA_WIDTH), F32)
    return pl.pallas_call(
        functools.partial(_hyena_filter_kernel, n=n),
        grid=(1,),
        in_specs=[_const_spec(s) for s in shapes],
        out_specs=[pl.BlockSpec((n, HYENA_WIDTH), lambda i: (0, 0))] * 3,
        out_shape=[out, out, out],
        compiler_params=_params("arbitrary"),
        name=f"hyena_filter_{n}",
    )(feats, w1, b1, w2, b2, w3, fr, win, fcat)


def _hyena_kernel(x_ref, cw_ref, bias_ref, fc_ref, p_ref, q_ref, r_ref, o_ref, w_s, x0_s, z_s, *, n):
    rows = min(n, TILE)
    n_tiles = n // rows
    inv_n = 1.0 / (2 * n)

    def conv_tile(i, carry):
        r0 = pl.multiple_of(i * rows, rows)
        x = x_ref[pl.ds(r0, rows), :].astype(F32)
        p0 = pl.multiple_of(jnp.maximum(r0 - 16, 0), 16)
        n0 = pl.multiple_of(jnp.minimum(r0 + rows, n - 16), 16)
        xp = x_ref[pl.ds(p0, 16), :].astype(F32)[15:16, :] * jnp.where(i == 0, 0.0, 1.0)
        xn = x_ref[pl.ds(n0, 16), :].astype(F32)[0:1, :] * jnp.where(i == n_tiles - 1, 0.0, 1.0)
        row = lax.broadcasted_iota(jnp.int32, (rows, 3 * HYENA_WIDTH), 0)
        xdn = jnp.where(row == 0, xp, pltpu.roll(x, 1, 0))
        xup = jnp.where(row == rows - 1, xn, pltpu.roll(x, rows - 1, 0))
        c = cw_ref[0:1, :] * xdn + cw_ref[1:2, :] * x + cw_ref[2:3, :] * xup
        x0_s[pl.ds(r0, rows), :] = c[:, 0:HYENA_WIDTH]
        w_s[pl.ds(r0, rows), :] = c[:, 2 * HYENA_WIDTH:] * c[:, HYENA_WIDTH:2 * HYENA_WIDTH]
        return carry

    lax.fori_loop(0, n_tiles, conv_tile, 0)

    w = w_s[...]
    wb = w.astype(BF16)
    u_re = _dot(fc_ref[:, 0:n], wb)
    u_im = _dot(fc_ref[:, n:2 * n], wb)
    row = lax.broadcasted_iota(jnp.int32, (n, HYENA_WIDTH), 0)
    sgn = jnp.where((row & 1) == 0, 1.0, -1.0)
    u_nyq = jnp.sum(sgn * w, axis=0, keepdims=True)
    u_im = jnp.where(row == 0, u_nyq, u_im)
    fscale = jnp.where(row == 0, inv_n, 2.0 * inv_n)
    z_re = u_re * p_ref[...] - u_im * q_ref[...]
    z_im = u_re * q_ref[...] + u_im * r_ref[...]
    z_s[0:n, :] = (z_re * fscale).astype(BF16)
    z_s[n:2 * n, :] = (z_im * fscale).astype(BF16)
    y = _dot(fc_ref[...], z_s[...]) + sgn * (z_im[0:1, :] * inv_n) + w * bias_ref[...]
    o_ref[...] = (y * x0_s[...]).astype(BF16)


def _hyena_call(n, tile_index, hy, conv_w, bias, fcat, hp, hq, hr):
    return pl.pallas_call(
        functools.partial(_hyena_kernel, n=n),
        grid=(BATCH,),
        in_specs=[
            pl.BlockSpec((None, n, 3 * HYENA_WIDTH), lambda b: (b, tile_index, 0)),
            _const_spec((3, 3 * HYENA_WIDTH)),
            _const_spec((1, HYENA_WIDTH)),
            _const_spec((n, 2 * n)),
            _const_spec((n, HYENA_WIDTH)), _const_spec((n, HYENA_WIDTH)), _const_spec((n, HYENA_WIDTH)),
        ],
        out_specs=pl.BlockSpec((None, n, HYENA_WIDTH), lambda b: (b, 0, 0)),
        out_shape=jax.ShapeDtypeStruct((BATCH, n, HYENA_WIDTH), BF16),
        scratch_shapes=[
            pltpu.VMEM((n, HYENA_WIDTH), F32),
            pltpu.VMEM((n, HYENA_WIDTH), F32),
            pltpu.VMEM((2 * n, HYENA_WIDTH), BF16),
        ],
        compiler_params=_params("arbitrary"),
        name=f"hyena_{n}",
    )(hy, conv_w, bias, fcat, hp, hq, hr)


def _block_kernel(h_ref, hp_ref, hn_ref, a_ref, ap_ref, an_ref, d_ref, dp_ref, dn_ref, y_ref, yp_ref, yn_ref,
                  m_ref, g_ref, wo_ref, wu_ref, cw_ref, wd_ref, fg_ref, o_ref, act_s, *, rows, final):
    t = pl.program_id(1)
    ext = rows + 2 * HALO

    def mixed(h, a, d, y):
        acc = _dot(a, wo_ref[0:512, :]) + _dot(d, wo_ref[512:768, :]) + _dot(y, wo_ref[768:1024, :])
        return h + m_ref[2:3, :] * acc

    h1 = mixed(h_ref[...], a_ref[...], d_ref[...], y_ref[...])
    h1_top = mixed(jnp.concatenate([hp_ref[...], hp_ref[...]], axis=0), ap_ref[...], dp_ref[...],
                   yp_ref[...])[HALO:2 * HALO, :]
    h1_bottom = mixed(jnp.concatenate([hn_ref[...], hn_ref[...]], axis=0), an_ref[...], dn_ref[...],
                      yn_ref[...])[0:HALO, :]
    gain, shift, scale = g_ref[...], m_ref[3:4, :], m_ref[4:5, :]
    u = jnp.concatenate([
        jnp.where(t == 0, 0.0, _norm_mod(h1_top, gain, shift, scale)).astype(BF16),
        _norm_mod(h1, gain, shift, scale).astype(BF16),
        jnp.where(t == pl.num_programs(1) - 1, 0.0, _norm_mod(h1_bottom, gain, shift, scale)).astype(BF16)],
        axis=0)
    for j in range(N_FF_CHUNKS):
        p = _dot(u, wu_ref[j])
        cw = cw_ref[j]
        c = (cw[0:1, :] * pltpu.roll(p, 1, 0) + cw[1:2, :] * p + cw[2:3, :] * pltpu.roll(p, ext - 1, 0))
        c = c[HALO:HALO + rows, :]
        act_s[:, j * FF_CHUNK:(j + 1) * FF_CHUNK] = (_silu(c[:, 0:FF_CHUNK]) * c[:, FF_CHUNK:]).astype(BF16)
    res = h1 + m_ref[5:6, :] * _dot(act_s[...], wd_ref[...])
    if final:
        ms = jnp.mean(res * res, axis=-1, keepdims=True)
        res = res * lax.rsqrt(ms + NORM_EPS) * fg_ref[...]
    o_ref[...] = res


def _block_call(h, attn, gdn, hy, mod, gain, w_out, w_up, conv_w, w_down, final_gain, *, rows, context, final):
    seg_rows = h.shape[1]
    base = SEQ if context else 0
    wide = 2 * HALO
    mod_row = (lambda b: CTX_MOD_ROW) if context else (lambda b: b)

    def main(width, offset):
        return pl.BlockSpec((None, rows, width), lambda b, t: (b, offset // rows + t, 0))

    def halos(width, block, offset, total):
        per, first, last = rows // block, offset // block, total // block - 1
        return [pl.BlockSpec((None, block, width), lambda b, t: (b, jnp.maximum(first + t * per - 1, 0), 0)),
                pl.BlockSpec((None, block, width), lambda b, t: (b, jnp.minimum(first + (t + 1) * per, last), 0))]

    return pl.pallas_call(
        functools.partial(_block_kernel, rows=rows, final=final),
        grid=(BATCH, seg_rows // rows),
        in_specs=[
            main(D_MODEL, 0), *halos(D_MODEL, HALO, 0, seg_rows),
            main(ATTN_WIDTH, base), *halos(ATTN_WIDTH, wide, base, STREAM),
            main(GDN_WIDTH, base), *halos(GDN_WIDTH, wide, base, STREAM),
            main(HYENA_WIDTH, 0), *halos(HYENA_WIDTH, wide, 0, seg_rows),
            pl.BlockSpec((None, 6, D_MODEL), lambda b, t: (mod_row(b), 0, 0)),
            _const_spec((1, D_MODEL)),
            _const_spec((D_MODEL, D_MODEL)),
            _const_spec((N_FF_CHUNKS, D_MODEL, 2 * FF_CHUNK)),
            _const_spec((N_FF_CHUNKS, 8, 2 * FF_CHUNK)),
            _const_spec((D_FF, D_MODEL)),
            _const_spec((1, D_MODEL)),
        ],
        out_specs=pl.BlockSpec((None, rows, D_MODEL), lambda b, t: (b, t, 0)),
        out_shape=jax.ShapeDtypeStruct(h.shape, F32),
        scratch_shapes=[pltpu.VMEM((rows, D_FF), BF16)],
        compiler_params=_params("arbitrary", "arbitrary"),
        name="block_ctx" if context else "block",
    )(h, h, h, attn, attn, attn, gdn, gdn, gdn, hy, hy, hy, mod, gain, w_out, w_up, conv_w, w_down, final_gain)


def _block_diag_ones(width, block):
    i = jnp.arange(width)
    return (i[:, None] // block == i[None, :] // block).astype(BF16)


def _rope_tables():
    n = jnp.arange(SEQ)
    lane = jnp.arange(LANES)
    d = lane % HEAD_DIM
    comp = jnp.where((d < HEAD_DIM // 2)[None, :], (n // GRID_W)[:, None], (n % GRID_W)[:, None]).astype(F32)
    inv = ROPE_THETA ** (-(d % 16).astype(F32) / 16.0)
    ang = comp * inv[None, :]
    sign = jnp.where((d % 32) < 16, -1.0, 1.0)[None, :]
    cos_t = jnp.concatenate([jnp.cos(ang), jnp.ones((CTX_LEN, LANES), F32)], axis=0)
    sin_t = jnp.concatenate([jnp.sin(ang) * sign, jnp.zeros((CTX_LEN, LANES), F32)], axis=0)
    return cos_t, sin_t


def _dft_table(n):
    f = jnp.arange(n, dtype=jnp.int32)
    m = (f[:, None] * f[None, :]) % (2 * n)
    ang = m.astype(F32) * (math.pi / n)
    return jnp.concatenate([jnp.cos(ang), -jnp.sin(ang)], axis=1).astype(BF16)


def _hyena_static(n):
    t = jnp.linspace(0.0, 1.0, n, dtype=F32)[:, None]
    omega = 2.0 * math.pi * jnp.arange(n, dtype=F32)[:, None] / n
    bands = jnp.linspace(1e-4, HYENA_EMB_BANDS - 1, HYENA_EMB_BANDS, dtype=F32)[None, :]
    feats = jnp.concatenate([t, jnp.cos(bands * omega), -jnp.sin(bands * omega)], axis=-1)
    feats = jnp.pad(feats, ((0, 0), (0, LANES - feats.shape[1])))
    max_decay = math.log(HYENA_DECAY_TARGET) / HYENA_FAST_DECAY
    min_decay = math.log(HYENA_DECAY_TARGET) / HYENA_SLOW_DECAY
    deltas = jnp.abs(jnp.linspace(min_decay, max_decay, HYENA_WIDTH, dtype=F32))
    window = jnp.exp(-t * deltas[None, :])
    return feats, window


def kernel(x, c, ctx, c_ctx, w_ada, b_ada, norm1, norm2, w_in, w_out, q_gain, k_gain, gdn_conv, gdn_a_log,
           gdn_dt_bias, gdn_norm, hyena_conv, hyena_w1, hyena_b1, hyena_w2, hyena_b2, hyena_w3, hyena_freq,
           hyena_bias, ffn_up, ffn_conv, ffn_down, final_norm):
    assert x.shape == (BATCH, SEQ, D_MODEL) and ctx.shape == (BATCH, CTX_LEN, D_MODEL)

    h_lat, h_ctx = x, ctx
    cvec = jnp.concatenate([c, c_ctx[None, :], jnp.zeros((MOD_ROWS - BATCH - 1, D_MODEL), F32)], axis=0)
    mod_all = _ada_call(cvec, w_ada, b_ada).reshape(DEPTH, MOD_ROWS, 6, D_MODEL)

    cos_t, sin_t = _rope_tables()
    seg_head = _block_diag_ones(LANES, HEAD_DIM)
    seg_gdn = _block_diag_ones(GDN_WIDTH, GDN_DK)
    ii = jnp.arange(GDN_GROUP * GDN_CHUNK)
    same_chunk = (ii[:, None] // GDN_CHUNK) == (ii[None, :] // GDN_CHUNK)
    tril = jnp.logical_and(same_chunk, ii[:, None] >= ii[None, :]).astype(BF16)
    triu = jnp.logical_and(same_chunk, ii[:, None] <= ii[None, :]).astype(BF16)
    chunk_ones = same_chunk.astype(BF16)
    col = jnp.arange(2 * GDN_WIDTH) // GDN_DK
    expand_g = (jnp.arange(LANES)[:, None] == 8 + col[None, :]).astype(BF16)
    expand_b = (jnp.arange(LANES)[:, None] == col[None, :]).astype(BF16)
    fcat_lat = _dft_table(SEQ)
    fcat_ctx = _dft_table(CTX_LEN)
    feats_lat, win_lat = _hyena_static(SEQ)
    feats_ctx, win_ctx = _hyena_static(CTX_LEN)

    for i in range(DEPTH):
        with_ctx = i < DEPTH - 1
        mod = mod_all[i]
        wi = w_in[i]
        w_cat = jnp.concatenate(
            [wi[:, 0:768], wi[:, 768:1536], wi[:, 1536:1792], wi[:, 1808:2576], wi[:, 1792:1808],
             jnp.zeros((D_MODEL, LANES - 16), F32)], axis=1).astype(BF16)
        q_pad, kt, v, gqkv, ggate, hyp, bd = _in_call(
            h_lat, h_ctx, mod, norm1[i][None, :], w_cat, cos_t, sin_t,
            jnp.tile(q_gain[i], 2)[None, :], jnp.tile(k_gain[i], 2)[None, :], seg_head)

        attn = _attn_call(q_pad, kt, v, with_ctx)

        pad8 = jnp.zeros((8,), F32)
        alog_row = jnp.concatenate([pad8, gdn_a_log[i].reshape(-1), jnp.zeros((LANES - 16,), F32)])[None, :]
        dtb_row = jnp.concatenate([pad8, gdn_dt_bias[i].reshape(-1), jnp.zeros((LANES - 16,), F32)])[None, :]
        gdn = _gdn_call(gqkv, ggate, bd, gdn_conv[i], alog_row, dtb_row, jnp.tile(gdn_norm[i], GDN_HEADS)[None, :],
                        seg_gdn, tril, triu, chunk_ones, expand_g, expand_b)

        w1 = jnp.pad(hyena_w1[i], ((0, LANES - hyena_w1.shape[1]), (0, 0))).astype(BF16)
        filt_args = (w1, hyena_b1[i][None, :], hyena_w2[i].astype(BF16), hyena_b2[i][None, :],
                     hyena_w3[i].astype(BF16), hyena_freq[i][None, :])
        hp, hq, hr = _hyena_filter_call(SEQ, feats_lat, *filt_args, win_lat, fcat_lat)
        hy_lat = _hyena_call(SEQ, 0, hyp, hyena_conv[i], hyena_bias[i], fcat_lat, hp, hq, hr)
        if with_ctx:
            hp, hq, hr = _hyena_filter_call(CTX_LEN, feats_ctx, *filt_args, win_ctx, fcat_ctx)
            hy_ctx = _hyena_call(CTX_LEN, CTX_TILE, hyp, hyena_conv[i], hyena_bias[i], fcat_ctx, hp, hq, hr)

        wu = ffn_up[i]
        w_up = jnp.concatenate([wu[:, 0:D_FF].reshape(D_MODEL, N_FF_CHUNKS, FF_CHUNK),
                                wu[:, D_FF:].reshape(D_MODEL, N_FF_CHUNKS, FF_CHUNK)], axis=2)
        w_up = jnp.transpose(w_up, (1, 0, 2)).astype(BF16)
        fc = ffn_conv[i]
        conv_w = jnp.concatenate([fc[:, 0:D_FF].reshape(3, N_FF_CHUNKS, FF_CHUNK),
                                  fc[:, D_FF:].reshape(3, N_FF_CHUNKS, FF_CHUNK)], axis=2)
        conv_w = jnp.pad(jnp.transpose(conv_w, (1, 0, 2)), ((0, 0), (0, 5), (0, 0)))
        weights = (mod, norm2[i][None, :], w_out[i].astype(BF16), w_up, conv_w, ffn_down[i].astype(BF16),
                   final_norm[None, :])
        h_lat = _block_call(h_lat, attn, gdn, hy_lat, *weights, rows=FFN_ROWS, context=False, final=not with_ctx)
        if with_ctx:
            h_ctx = _block_call(h_ctx, attn, gdn, hy_ctx, *weights, rows=CTX_LEN, context=True, final=False)

    return h_lat
```

```python
import functools
import math

import jax
import jax.numpy as jnp
from jax import lax
from jax.experimental import pallas as pl
from jax.experimental.pallas import tpu as pltpu

F32 = jnp.float32
BF16 = jnp.bfloat16

D_MODEL = 1024
BATCH = 16
SEQ = 2048
DEPTH = 4
CTX_LEN = 256
STREAM = SEQ + CTX_LEN
TILE = 256
N_TILES = STREAM // TILE
N_LAT_TILES = SEQ // TILE
CTX_TILE = N_LAT_TILES
GRID_W = 64
NORM_EPS = 1e-6
LOG2_E = math.log2(math.e)
HEAD_DIM = 64
LANES = 128
MOD_ROWS = 24
CTX_MOD_ROW = BATCH

ATTN_WIDTH = 512
ATTN_HEADS = 8
ATTN_KV_WIDTH = 128
ROPE_THETA = 10000.0

GDN_WIDTH = 256
GDN_HEADS = 4
GDN_DK = 64
GDN_QKV_WIDTH = 768
GDN_CHUNK = 64
N_CHUNKS = STREAM // GDN_CHUNK
N_LAT_CHUNKS = SEQ // GDN_CHUNK
N_CTX_CHUNKS = CTX_LEN // GDN_CHUNK
GDN_GROUP = 4

HYENA_WIDTH = 256
HYENA_EMB_BANDS = 16
HYENA_FILTER_HIDDEN = 64
HYENA_FAST_DECAY = 0.3
HYENA_SLOW_DECAY = 1.5
HYENA_DECAY_TARGET = 1e-2

D_FF = 2816
FF_CHUNK = 256
N_FF_CHUNKS = D_FF // FF_CHUNK
HALO = 8
FFN_ROWS = 512

COL_ATTN = (0, 768)
COL_GQKV = (768, 1536)
COL_GGATE = (1536, 1792)
COL_HY = (1792, 2560)
COL_BD = (2560, 2688)
IN_COLS = 2688

VMEM_LIMIT = 56 * 1024 * 1024


def _dot(a, b):
    return jnp.dot(a, b, preferred_element_type=F32)


def _dot_nt(a, b):
    return lax.dot_general(a, b, (((1,), (1,)), ((), ())), preferred_element_type=F32)


def _dot_tn(a, b):
    return lax.dot_general(a, b, (((0,), (0,)), ((), ())), preferred_element_type=F32)


def _sigmoid(x):
    return 1.0 / (1.0 + jnp.exp(-x))


def _silu(x):
    return x * _sigmoid(x)


def _softplus(x):
    return jnp.maximum(x, 0.0) + jnp.log(1.0 + jnp.exp(-jnp.abs(x)))


def _seg_sum(x, seg):
    hi = x.astype(BF16)
    lo = (x - hi.astype(F32)).astype(BF16)
    return _dot(hi, seg) + _dot(lo, seg)


def _tri_dot(tri, x):
    x1 = x.astype(BF16)
    r1 = x - x1.astype(F32)
    x2 = r1.astype(BF16)
    x3 = (r1 - x2.astype(F32)).astype(BF16)
    return _dot(tri, x1) + _dot(tri, x2) + _dot(tri, x3)


def _dot_split(x, sel, terms):
    out = None
    for _ in range(terms):
        piece = x.astype(BF16)
        x = x - piece.astype(F32)
        out = _dot(piece, sel) if out is None else out + _dot(piece, sel)
    return out


def _params(*sem):
    return pltpu.CompilerParams(dimension_semantics=sem, vmem_limit_bytes=VMEM_LIMIT)


def _const_spec(shape):
    nd = len(shape)
    return pl.BlockSpec(shape, lambda *_: (0,) * nd, pipeline_mode=pl.Buffered(1))


def _mod_row(b, t):
    return jnp.where(t == CTX_TILE, CTX_MOD_ROW, b)


def _ada_kernel(c_ref, w_ref, b_ref, o_ref):
    s = _silu(c_ref[...]).astype(BF16)
    o_ref[...] = _dot(s, w_ref[...].astype(BF16)) + b_ref[...]


def _ada_call(cvec, w_ada, b_ada):
    nj = 6
    return pl.pallas_call(
        _ada_kernel,
        grid=(DEPTH, nj),
        in_specs=[
            pl.BlockSpec((MOD_ROWS, D_MODEL), lambda i, j: (0, 0)),
            pl.BlockSpec((None, D_MODEL, D_MODEL), lambda i, j: (i, 0, j)),
            pl.BlockSpec((None, 1, D_MODEL), lambda i, j: (i, 0, j)),
        ],
        out_specs=pl.BlockSpec((None, MOD_ROWS, D_MODEL), lambda i, j: (i, 0, j)),
        out_shape=jax.ShapeDtypeStruct((DEPTH, MOD_ROWS, 6 * D_MODEL), F32),
        compiler_params=_params("arbitrary", "arbitrary"),
        name="adaln",
    )(cvec, w_ada, b_ada.reshape(DEPTH, 1, 6 * D_MODEL))


def _norm_mod(x, gain, shift, scale):
    ms = jnp.mean(x * x, axis=-1, keepdims=True)
    return (x * lax.rsqrt(ms + NORM_EPS) * gain) * (1.0 + scale) + shift


def _in_kernel(xl_ref, xc_ref, m_ref, g_ref, w_ref, cos_ref, sin_ref, qg_ref, kg_ref, seg_ref,
               q_ref, kt_ref, v_ref, ogq_ref, ogg_ref, ohy_ref, obd_ref):
    x = jnp.where(pl.program_id(1) == CTX_TILE, xc_ref[...], xl_ref[...])
    u = _norm_mod(x, g_ref[...], m_ref[0:1, :], m_ref[1:2, :]).astype(BF16)
    ogq_ref[...] = _dot(u, w_ref[:, COL_GQKV[0]:COL_GQKV[1]]).astype(BF16)
    ogg_ref[...] = _dot(u, w_ref[:, COL_GGATE[0]:COL_GGATE[1]]).astype(BF16)
    ohy_ref[...] = _dot(u, w_ref[:, COL_HY[0]:COL_HY[1]]).astype(BF16)
    obd_ref[...] = _dot(u, w_ref[:, COL_BD[0]:COL_BD[1]])

    a = _dot(u, w_ref[:, COL_ATTN[0]:COL_ATTN[1]])
    lane = lax.broadcasted_iota(jnp.int32, (TILE, LANES), 1)
    first = (lane % 32) < 16
    low = lane < HEAD_DIM
    seg = seg_ref[...]
    cosv = cos_ref[...]
    sinv = sin_ref[...]

    def norm_rope(x, gain):
        ms = _seg_sum(x * x, seg) * (1.0 / HEAD_DIM)
        y = x * lax.rsqrt(ms + NORM_EPS) * gain
        partner = jnp.where(first, pltpu.roll(y, LANES - 16, 1), pltpu.roll(y, 16, 1))
        return y * cosv + partner * sinv

    for j in range(ATTN_HEADS // 2):
        r = norm_rope(a[:, j * LANES:(j + 1) * LANES], qg_ref[...]) * (HEAD_DIM ** -0.5 * LOG2_E)
        swapped = pltpu.roll(r, HEAD_DIM, 1)
        if j // 2 == 0:
            qa = jnp.where(low, r, 0.0)
            qb = jnp.where(low, swapped, 0.0)
        else:
            qa = jnp.where(low, 0.0, swapped)
            qb = jnp.where(low, 0.0, r)
        q_ref[:, (2 * j) * LANES:(2 * j + 1) * LANES] = qa.astype(BF16)
        q_ref[:, (2 * j + 1) * LANES:(2 * j + 2) * LANES] = qb.astype(BF16)
    rk = norm_rope(a[:, ATTN_WIDTH:ATTN_WIDTH + ATTN_KV_WIDTH], kg_ref[...])
    kt_ref[...] = rk.T.astype(BF16)
    v_ref[...] = a[:, ATTN_WIDTH + ATTN_KV_WIDTH:].astype(BF16)


def _in_call(h_lat, h_ctx, mod, gain, w_cat, cos_t, sin_t, qg, kg, seg):
    def tile(width):
        return pl.BlockSpec((None, TILE, width), lambda b, t: (b, t, 0))
    return pl.pallas_call(
        _in_kernel,
        grid=(BATCH, N_TILES),
        in_specs=[
            pl.BlockSpec((None, TILE, D_MODEL), lambda b, t: (b, jnp.minimum(t, N_LAT_TILES - 1), 0)),
            pl.BlockSpec((None, TILE, D_MODEL), lambda b, t: (b, 0, 0)),
            pl.BlockSpec((None, 6, D_MODEL), lambda b, t: (_mod_row(b, t), 0, 0)),
            _const_spec((1, D_MODEL)),
            _const_spec((D_MODEL, IN_COLS)),
            pl.BlockSpec((TILE, LANES), lambda b, t: (t, 0)),
            pl.BlockSpec((TILE, LANES), lambda b, t: (t, 0)),
            _const_spec((1, LANES)),
            _const_spec((1, LANES)),
            _const_spec((LANES, LANES)),
        ],
        out_specs=[
            tile(ATTN_HEADS * LANES),
            pl.BlockSpec((None, LANES, TILE), lambda b, t: (b, 0, t)),
            tile(LANES), tile(768), tile(256), tile(768), tile(LANES)],
        out_shape=[
            jax.ShapeDtypeStruct((BATCH, STREAM, ATTN_HEADS * LANES), BF16),
            jax.ShapeDtypeStruct((BATCH, LANES, STREAM), BF16),
            jax.ShapeDtypeStruct((BATCH, STREAM, LANES), BF16),
            jax.ShapeDtypeStruct((BATCH, STREAM, 768), BF16),
            jax.ShapeDtypeStruct((BATCH, STREAM, 256), BF16),
            jax.ShapeDtypeStruct((BATCH, STREAM, 768), BF16),
            jax.ShapeDtypeStruct((BATCH, STREAM, LANES), F32),
        ],
        compiler_params=_params("arbitrary", "arbitrary"),
        name="in_proj",
    )(h_lat, h_ctx, mod, gain, w_cat, cos_t, sin_t, qg, kg, seg)


def _attn_kernel(q_ref, kt_ref, v_ref, o_ref, *, with_ctx):
    low = lax.broadcasted_iota(jnp.int32, (TILE, LANES), 1) < HEAD_DIM

    def run(k0, k1):
        kt = kt_ref[:, k0:k1]
        v = v_ref[k0:k1, :]
        s_next = _dot(q_ref[:, 0:LANES], kt)
        outs = []
        for hq in range(ATTN_HEADS):
            s = s_next
            if hq + 1 < ATTN_HEADS:
                s_next = _dot(q_ref[:, (hq + 1) * LANES:(hq + 2) * LANES], kt)
            p = jnp.exp2(s - jnp.max(s, axis=-1, keepdims=True))
            den = jnp.sum(p, axis=-1, keepdims=True)
            outs.append(_dot(p.astype(BF16), v) / den)
            if hq % 2 == 1:
                j = hq // 2
                if j // 2 == 0:
                    blk = jnp.where(low, outs[0], pltpu.roll(outs[1], HEAD_DIM, 1))
                else:
                    blk = jnp.where(low, pltpu.roll(outs[0], HEAD_DIM, 1), outs[1])
                o_ref[:, j * LANES:(j + 1) * LANES] = blk.astype(BF16)
                outs = []

    if with_ctx:
        t = pl.program_id(1)

        @pl.when(t < CTX_TILE)
        def _():
            run(0, STREAM)

        @pl.when(t == CTX_TILE)
        def _():
            run(SEQ, STREAM)
    else:
        run(0, STREAM)


def _attn_call(q_pad, kt, v, with_ctx):
    nt = N_TILES if with_ctx else N_LAT_TILES
    return pl.pallas_call(
        functools.partial(_attn_kernel, with_ctx=with_ctx),
        grid=(BATCH, nt),
        in_specs=[
            pl.BlockSpec((None, TILE, ATTN_HEADS * LANES), lambda b, t: (b, t, 0)),
            pl.BlockSpec((None, LANES, STREAM), lambda b, t: (b, 0, 0)),
            pl.BlockSpec((None, STREAM, LANES), lambda b, t: (b, 0, 0)),
        ],
        out_specs=pl.BlockSpec((None, TILE, ATTN_WIDTH), lambda b, t: (b, t, 0)),
        out_shape=jax.ShapeDtypeStruct((BATCH, STREAM, ATTN_WIDTH), BF16),
        compiler_params=_params("arbitrary", "arbitrary"),
        name="attention",
    )(q_pad, kt, v)


def _gdn_kernel(gq_ref, gg_ref, bd_ref, cw_ref, alog_ref, dtb_ref, ng_ref, seg_ref, tril_ref, triu_ref, ones_ref,
                eg_ref, eb_ref, o_ref, q_s, k_s, v_s, g_s, b_s, u_s, w_s, qk_s, qd_s, kd_s, ge_s, st_s, o_s):
    C = GDN_CHUNK
    seg = seg_ref[...]

    def prep_tile(i, carry):
        r0 = pl.multiple_of(i * TILE, TILE)
        x = gq_ref[pl.ds(r0, TILE), :].astype(F32)
        p0 = pl.multiple_of(jnp.maximum(r0 - 16, 0), 16)
        n0 = pl.multiple_of(jnp.minimum(r0 + TILE, STREAM - 16), 16)
        first_of_segment = jnp.logical_or(i == 0, i == CTX_TILE)
        last_of_segment = jnp.logical_or(i == N_LAT_TILES - 1, i == CTX_TILE)
        xp = gq_ref[pl.ds(p0, 16), :].astype(F32)[15:16, :] * jnp.where(first_of_segment, 0.0, 1.0)
        xn = gq_ref[pl.ds(n0, 16), :].astype(F32)[0:1, :] * jnp.where(last_of_segment, 0.0, 1.0)
        row = lax.broadcasted_iota(jnp.int32, (TILE, GDN_QKV_WIDTH), 0)
        xdn = jnp.where(row == 0, xp, pltpu.roll(x, 1, 0))
        xup = jnp.where(row == TILE - 1, xn, pltpu.roll(x, TILE - 1, 0))
        c = _silu(cw_ref[0:1, :] * xdn + cw_ref[1:2, :] * x + cw_ref[2:3, :] * xup)
        q = c[:, 0:256]
        k = c[:, 256:512]
        q = q * lax.rsqrt(_seg_sum(q * q, seg) + NORM_EPS)
        k = k * lax.rsqrt(_seg_sum(k * k, seg) + NORM_EPS)
        q_s[pl.ds(r0, TILE), :] = q * (GDN_DK ** -0.5)
        k_s[pl.ds(r0, TILE), :] = k
        v_s[pl.ds(r0, TILE), :] = c[:, 512:768]
        bd = bd_ref[pl.ds(r0, TILE), :]
        b_s[pl.ds(r0, TILE), :] = _sigmoid(bd)
        g_s[pl.ds(r0, TILE), :] = -jnp.exp(alog_ref[...]) * _softplus(bd + dtb_ref[...])
        return carry

    lax.fori_loop(0, N_TILES, prep_tile, 0)

    G = GDN_GROUP
    R = G * C
    ri = lax.broadcasted_iota(jnp.int32, (C, GDN_WIDTH), 0)
    cj = lax.broadcasted_iota(jnp.int32, (C, GDN_WIDTH), 1) % C
    lane = lax.broadcasted_iota(jnp.int32, (R, LANES), 1)
    eye4 = jnp.where(ri == cj, 1.0, 0.0)
    same16 = (ri // 16) == (cj // 16)
    off32 = jnp.logical_and((ri // 32) == (cj // 32), jnp.logical_not(same16))
    off64 = (ri // 32) != (cj // 32)
    incl = (ri >= cj, ri <= cj)
    strict = (ri > cj, ri < cj)

    def block_diag(z):
        zb = z.astype(BF16)
        return jnp.concatenate([zb, zb, zb, zb], axis=0) * seg

    def phase1(it, carry):
        c0 = it * G
        r0 = pl.multiple_of(c0 * C, R)
        kc = k_s[pl.ds(r0, R), :]
        qc = q_s[pl.ds(r0, R), :]
        vc = v_s[pl.ds(r0, R), :]
        gt = g_s[pl.ds(r0, R), :]
        bt = b_s[pl.ds(r0, R), :]
        g_fwd = _tri_dot(tril_ref[...], gt)
        g_bwd = _tri_dot(triu_ref[...], gt)
        gc = jnp.where(lane < 8 + GDN_HEADS, g_fwd, g_bwd)
        gx = _dot_split(gc, eg_ref[...], 3)
        bx = _dot_split(bt, eb_ref[...], 2)
        gtx = jnp.concatenate([
            jnp.concatenate([jnp.broadcast_to(gx[g * C + C - 1:g * C + C, 0:GDN_WIDTH], (C, GDN_WIDTH)),
                             jnp.broadcast_to(gx[g * C:g * C + 1, GDN_WIDTH:], (C, GDN_WIDTH))], axis=1)
            for g in range(G)], axis=0)
        e_g = jnp.exp(gx)
        e_gk = jnp.exp(gtx - gx)
        eye_rows = jnp.concatenate([jnp.concatenate([eye4, eye4], axis=1)] * G, axis=0)
        grow = _tri_dot(ones_ref[...], jnp.where(eye_rows > 0.0, gx, 0.0))
        kb = kc.astype(BF16)
        qb = qc.astype(BF16)
        kk, qk0 = [], []
        for g in range(G):
            rs = slice(g * C, (g + 1) * C)
            bdk = jnp.concatenate([kb[rs]] * 4, axis=0) * seg
            kk.append(_dot_nt(kb[rs], bdk))
            qk0.append(_dot_nt(qb[rs], bdk))
        probs = [(g, d) for g in range(G) for d in range(2)]

        def part(arr, g, d):
            return arr[g * C:(g + 1) * C, d * GDN_WIDTH:(d + 1) * GDN_WIDTH]

        decay = [jnp.where(incl[d], jnp.exp(part(gx, g, d) - part(grow, g, d)), 0.0) for g, d in probs]
        a = [jnp.where(strict[d], part(bx, g, d) * kk[g] * dec, 0.0) for (g, d), dec in zip(probs, decay)]
        x = [jnp.where(same16, -ai, 0.0) for ai in a]
        inv = [eye4 + xi for xi in x]
        y = [_dot(xi.astype(BF16), block_diag(xi)) for xi in x]
        for stage in range(3):
            if stage < 2:
                prod = [_dot(yi.astype(BF16), jnp.concatenate([block_diag(ii), block_diag(yi)], axis=1))
                        for yi, ii in zip(y, inv)]
                inv = [ii + pi[:, 0:GDN_WIDTH] for ii, pi in zip(inv, prod)]
                y = [pi[:, GDN_WIDTH:] for pi in prod]
            else:
                inv = [ii + _dot(yi.astype(BF16), block_diag(ii)) for yi, ii in zip(y, inv)]
        for off in (off32, off64):
            t = [_dot(ii.astype(BF16), block_diag(jnp.where(off, ai, 0.0))) for ii, ai in zip(inv, a)]
            inv = [ii - _dot(ti.astype(BF16), block_diag(ii)) for ii, ti in zip(inv, t)]
        for (g, d), ii, dec in zip(probs, inv, decay):
            rs = slice(g * C, (g + 1) * C)
            bxi = part(bx, g, d)
            egi = part(e_g, g, d)
            rhs = jnp.concatenate([block_diag(vc[rs] * bxi), block_diag(kc[rs] * (bxi * egi))], axis=1)
            r = _dot(ii.astype(BF16), rhs)
            u_s[c0 + g, d] = r[:, 0:GDN_WIDTH]
            w_s[c0 + g, d] = r[:, GDN_WIDTH:].astype(BF16)
            qk_s[c0 + g, d] = (qk0[g] * dec).astype(BF16)
            qd_s[c0 + g, d] = (qc[rs] * egi).astype(BF16)
            kd_s[c0 + g, d] = (kc[rs] * part(e_gk, g, d)).astype(BF16)
            ge_s[c0 + g, d] = jnp.exp(part(gtx, g, d)[0:8, :])
        return carry

    lax.fori_loop(0, N_CHUNKS // G, phase1, 0)

    st_s[...] = jnp.zeros(st_s.shape, F32)
    o_s[...] = jnp.zeros(o_s.shape, F32)
    bi = lax.broadcasted_iota(jnp.int32, (GDN_WIDTH, GDN_WIDTH), 0) // C
    bj = lax.broadcasted_iota(jnp.int32, (GDN_WIDTH, GDN_WIDTH), 1) // C
    on_diag = bi == bj

    def phase2(s, carry):
        cs = (jnp.where(s < N_CTX_CHUNKS, s + N_LAT_CHUNKS, s - N_CTX_CHUNKS), N_CHUNKS - 1 - s)
        st = [st_s[d] for d in range(2)]
        stb = [x.astype(BF16) for x in st]
        r1 = [_dot(jnp.concatenate([w_s[cs[d], d], qd_s[cs[d], d]], axis=0), stb[d]) for d in range(2)]
        vn = [u_s[cs[d], d] - r1[d][0:C, :] for d in range(2)]
        vnb = [x.astype(BF16) for x in vn]
        o = [r1[d][C:2 * C, :] + _dot(qk_s[cs[d], d], jnp.concatenate([vnb[d]] * 4, axis=0) * seg)
             for d in range(2)]
        upd = [_dot_tn(kd_s[cs[d], d], vnb[d]) for d in range(2)]
        for d in range(2):
            st_s[d] = st[d] * ge_s[cs[d], d][0:1, :] + jnp.where(on_diag, upd[d], 0.0)
            r0 = pl.multiple_of(cs[d] * C, C)
            o_s[pl.ds(r0, C), :] += o[d]
        return carry

    lax.fori_loop(0, N_CHUNKS, phase2, 0)

    def finish_tile(i, carry):
        r0 = pl.multiple_of(i * TILE, TILE)
        o = o_s[pl.ds(r0, TILE), :]
        ms = _seg_sum(o * o, seg) * (1.0 / GDN_DK)
        y = o * lax.rsqrt(ms + NORM_EPS) * ng_ref[...]
        o_ref[pl.ds(r0, TILE), :] = (y * _silu(gg_ref[pl.ds(r0, TILE), :].astype(F32))).astype(BF16)
        return carry

    lax.fori_loop(0, N_TILES, finish_tile, 0)


def _gdn_call(gqkv, ggate, bd, conv_w, alog_row, dtb_row, ng_row, seg, tril, triu, ones, expand_g, expand_b):
    def per_batch(width):
        return pl.BlockSpec((None, STREAM, width), lambda b: (b, 0, 0))
    rows = GDN_GROUP * GDN_CHUNK
    packed = (N_CHUNKS, 2, GDN_CHUNK, GDN_WIDTH)
    return pl.pallas_call(
        _gdn_kernel,
        grid=(BATCH,),
        in_specs=[
            per_batch(768), per_batch(256), per_batch(LANES),
            _const_spec((3, 768)), _const_spec((1, LANES)), _const_spec((1, LANES)), _const_spec((1, GDN_WIDTH)),
            _const_spec((GDN_WIDTH, GDN_WIDTH)), _const_spec((rows, rows)), _const_spec((rows, rows)),
            _const_spec((rows, rows)), _const_spec((LANES, 2 * GDN_WIDTH)), _const_spec((LANES, 2 * GDN_WIDTH)),
        ],
        out_specs=pl.BlockSpec((None, STREAM, GDN_WIDTH), lambda b: (b, 0, 0)),
        out_shape=jax.ShapeDtypeStruct((BATCH, STREAM, GDN_WIDTH), BF16),
        scratch_shapes=[
            pltpu.VMEM((STREAM, GDN_WIDTH), F32),
            pltpu.VMEM((STREAM, GDN_WIDTH), F32),
            pltpu.VMEM((STREAM, GDN_WIDTH), F32),
            pltpu.VMEM((STREAM, LANES), F32),
            pltpu.VMEM((STREAM, LANES), F32),
            pltpu.VMEM(packed, F32),
            pltpu.VMEM(packed, BF16),
            pltpu.VMEM(packed, BF16),
            pltpu.VMEM(packed, BF16),
            pltpu.VMEM(packed, BF16),
            pltpu.VMEM((N_CHUNKS, 2, 8, GDN_WIDTH), F32),
            pltpu.VMEM((2, GDN_WIDTH, GDN_WIDTH), F32),
            pltpu.VMEM((STREAM, GDN_WIDTH), F32),
        ],
        compiler_params=_params("arbitrary"),
        name="gdn",
    )(gqkv, ggate, bd, conv_w, alog_row, dtb_row, ng_row, seg, tril, triu, ones, expand_g, expand_b)


def _hyena_filter_kernel(feat_ref, w1_ref, b1_ref, w2_ref, b2_ref, w3_ref, fr_ref, win_ref, fc_ref,
                         p_ref, q_ref, r_ref, *, n):
    fr = fr_ref[...]
    hid = jnp.sin(fr * (_dot(feat_ref[...].astype(BF16), w1_ref[...]) + b1_ref[...]))
    hid = jnp.sin(fr * (_dot(hid.astype(BF16), w2_ref[...]) + b2_ref[...]))
    filt = _dot(hid.astype(BF16), w3_ref[...])
    win = win_ref[...]
    row = lax.broadcasted_iota(jnp.int32, (n, HYENA_WIDTH), 0)
    h_fwd = filt[:, 0:HYENA_WIDTH] * win
    h_bwd = jnp.where(row == 0, 0.0, filt[:, HYENA_WIDTH:] * win)
    hcat = jnp.concatenate([h_fwd, h_bwd], axis=1).astype(BF16)
    h_re = _dot(fc_ref[:, 0:n], hcat)
    h_im = _dot(fc_ref[:, n:2 * n], hcat)
    sgn = jnp.where((row & 1) == 0, 1.0, -1.0)
    nyq = jnp.sum(sgn * (h_fwd + h_bwd), axis=0, keepdims=True)
    re = h_re[:, 0:HYENA_WIDTH] + h_re[:, HYENA_WIDTH:]
    p_ref[...] = re
    q_ref[...] = h_im[:, 0:HYENA_WIDTH] - h_im[:, HYENA_WIDTH:]
    r_ref[...] = jnp.where(row == 0, nyq, re)


def _hyena_filter_call(n, feats, w1, b1, w2, b2, w3, fr, win, fcat):
    shapes = [(n, LANES), (LANES, HYENA_FILTER_HIDDEN), (1, HYENA_FILTER_HIDDEN),
              (HYENA_FILTER_HIDDEN, HYENA_FILTER_HIDDEN), (1, HYENA_FILTER_HIDDEN),
              (HYENA_FILTER_HIDDEN, 2 * HYENA_WIDTH), (1, HYENA_FILTER_HIDDEN), (n, HYENA_WIDTH), (n, 2 * n)]
    out = jax.ShapeDtypeStruct((n, HYENA_WIDTH), F32)
    return pl.pallas_call(
        functools.partial(_hyena_filter_kernel, n=n),
        grid=(1,),
        in_specs=[_const_spec(s) for s in shapes],
        out_specs=[pl.BlockSpec((n, HYENA_WIDTH), lambda i: (0, 0))] * 3,
        out_shape=[out, out, out],
        compiler_params=_params("arbitrary"),
        name=f"hyena_filter_{n}",
    )(feats, w1, b1, w2, b2, w3, fr, win, fcat)


def _hyena_kernel(x_ref, cw_ref, bias_ref, fc_ref, p_ref, q_ref, r_ref, o_ref, w_s, x0_s, z_s, *, n):
    rows = min(n, TILE)
    n_tiles = n // rows
    inv_n = 1.0 / (2 * n)

    def conv_tile(i, carry):
        r0 = pl.multiple_of(i * rows, rows)
        x = x_ref[pl.ds(r0, rows), :].astype(F32)
        p0 = pl.multiple_of(jnp.maximum(r0 - 16, 0), 16)
        n0 = pl.multiple_of(jnp.minimum(r0 + rows, n - 16), 16)
        xp = x_ref[pl.ds(p0, 16), :].astype(F32)[15:16, :] * jnp.where(i == 0, 0.0, 1.0)
        xn = x_ref[pl.ds(n0, 16), :].astype(F32)[0:1, :] * jnp.where(i == n_tiles - 1, 0.0, 1.0)
        row = lax.broadcasted_iota(jnp.int32, (rows, 3 * HYENA_WIDTH), 0)
        xdn = jnp.where(row == 0, xp, pltpu.roll(x, 1, 0))
        xup = jnp.where(row == rows - 1, xn, pltpu.roll(x, rows - 1, 0))
        c = cw_ref[0:1, :] * xdn + cw_ref[1:2, :] * x + cw_ref[2:3, :] * xup
        x0_s[pl.ds(r0, rows), :] = c[:, 0:HYENA_WIDTH]
        w_s[pl.ds(r0, rows), :] = c[:, 2 * HYENA_WIDTH:] * c[:, HYENA_WIDTH:2 * HYENA_WIDTH]
        return carry

    lax.fori_loop(0, n_tiles, conv_tile, 0)

    w = w_s[...]
    wb = w.astype(BF16)
    u_re = _dot(fc_ref[:, 0:n], wb)
    u_im = _dot(fc_ref[:, n:2 * n], wb)
    row = lax.broadcasted_iota(jnp.int32, (n, HYENA_WIDTH), 0)
    sgn = jnp.where((row & 1) == 0, 1.0, -1.0)
    u_nyq = jnp.sum(sgn * w, axis=0, keepdims=True)
    u_im = jnp.where(row == 0, u_nyq, u_im)
    fscale = jnp.where(row == 0, inv_n, 2.0 * inv_n)
    z_re = u_re * p_ref[...] - u_im * q_ref[...]
    z_im = u_re * q_ref[...] + u_im * r_ref[...]
    z_s[0:n, :] = (z_re * fscale).astype(BF16)
    z_s[n:2 * n, :] = (z_im * fscale).astype(BF16)
    y = _dot(fc_ref[...], z_s[...]) + sgn * (z_im[0:1, :] * inv_n) + w * bias_ref[...]
    o_ref[...] = (y * x0_s[...]).astype(BF16)


def _hyena_call(n, tile_index, hy, conv_w, bias, fcat, hp, hq, hr):
    return pl.pallas_call(
        functools.partial(_hyena_kernel, n=n),
        grid=(BATCH,),
        in_specs=[
            pl.BlockSpec((None, n, 3 * HYENA_WIDTH), lambda b: (b, tile_index, 0)),
            _const_spec((3, 3 * HYENA_WIDTH)),
            _const_spec((1, HYENA_WIDTH)),
            _const_spec((n, 2 * n)),
            _const_spec((n, HYENA_WIDTH)), _const_spec((n, HYENA_WIDTH)), _const_spec((n, HYENA_WIDTH)),
        ],
        out_specs=pl.BlockSpec((None, n, HYENA_WIDTH), lambda b: (b, 0, 0)),
        out_shape=jax.ShapeDtypeStruct((BATCH, n, HYENA_WIDTH), BF16),
        scratch_shapes=[
            pltpu.VMEM((n, HYENA_WIDTH), F32),
            pltpu.VMEM((n, HYENA_WIDTH), F32),
            pltpu.VMEM((2 * n, HYENA_WIDTH), BF16),
        ],
        compiler_params=_params("arbitrary"),
        name=f"hyena_{n}",
    )(hy, conv_w, bias, fcat, hp, hq, hr)


def _block_kernel(h_ref, hp_ref, hn_ref, a_ref, ap_ref, an_ref, d_ref, dp_ref, dn_ref, y_ref, yp_ref, yn_ref,
                  m_ref, g_ref, wo_ref, wu_ref, cw_ref, wd_ref, fg_ref, o_ref, act_s, *, rows, final):
    t = pl.program_id(1)
    ext = rows + 2 * HALO

    def mixed(h, a, d, y):
        acc = _dot(a, wo_ref[0:512, :]) + _dot(d, wo_ref[512:768, :]) + _dot(y, wo_ref[768:1024, :])
        return h + m_ref[2:3, :] * acc

    h1 = mixed(h_ref[...], a_ref[...], d_ref[...], y_ref[...])
    h1_top = mixed(jnp.concatenate([hp_ref[...], hp_ref[...]], axis=0), ap_ref[...], dp_ref[...],
                   yp_ref[...])[HALO:2 * HALO, :]
    h1_bottom = mixed(jnp.concatenate([hn_ref[...], hn_ref[...]], axis=0), an_ref[...], dn_ref[...],
                      yn_ref[...])[0:HALO, :]
    gain, shift, scale = g_ref[...], m_ref[3:4, :], m_ref[4:5, :]
    u = jnp.concatenate([
        jnp.where(t == 0, 0.0, _norm_mod(h1_top, gain, shift, scale)).astype(BF16),
        _norm_mod(h1, gain, shift, scale).astype(BF16),
        jnp.where(t == pl.num_programs(1) - 1, 0.0, _norm_mod(h1_bottom, gain, shift, scale)).astype(BF16)],
        axis=0)
    def up_conv(col):
        cols = slice(col, col + FF_CHUNK)
        p = _dot(u, wu_ref[:, cols])
        c = (cw_ref[0:1, cols] * pltpu.roll(p, 1, 0) + cw_ref[1:2, cols] * p
             + cw_ref[2:3, cols] * pltpu.roll(p, ext - 1, 0))
        return c[HALO:HALO + rows, :]

    for j in range(N_FF_CHUNKS):
        gate = up_conv(j * FF_CHUNK)
        value = up_conv(D_FF + j * FF_CHUNK)
        act_s[:, j * FF_CHUNK:(j + 1) * FF_CHUNK] = (_silu(gate) * value).astype(BF16)
    res = h1 + m_ref[5:6, :] * _dot(act_s[...], wd_ref[...])
    if final:
        ms = jnp.mean(res * res, axis=-1, keepdims=True)
        res = res * lax.rsqrt(ms + NORM_EPS) * fg_ref[...]
    o_ref[...] = res


def _block_call(h, attn, gdn, hy, mod, gain, w_out, w_up, conv_w, w_down, final_gain, *, rows, context, final):
    seg_rows = h.shape[1]
    base = SEQ if context else 0
    wide = 2 * HALO
    mod_row = (lambda b: CTX_MOD_ROW) if context else (lambda b: b)

    def main(width, offset):
        return pl.BlockSpec((None, rows, width), lambda b, t: (b, offset // rows + t, 0))

    def halos(width, block, offset, total):
        per, first, last = rows // block, offset // block, total // block - 1
        return [pl.BlockSpec((None, block, width), lambda b, t: (b, jnp.maximum(first + t * per - 1, 0), 0)),
                pl.BlockSpec((None, block, width), lambda b, t: (b, jnp.minimum(first + (t + 1) * per, last), 0))]

    return pl.pallas_call(
        functools.partial(_block_kernel, rows=rows, final=final),
        grid=(BATCH, seg_rows // rows),
        in_specs=[
            main(D_MODEL, 0), *halos(D_MODEL, HALO, 0, seg_rows),
            main(ATTN_WIDTH, base), *halos(ATTN_WIDTH, wide, base, STREAM),
            main(GDN_WIDTH, base), *halos(GDN_WIDTH, wide, base, STREAM),
            main(HYENA_WIDTH, 0), *halos(HYENA_WIDTH, wide, 0, seg_rows),
            pl.BlockSpec((None, 6, D_MODEL), lambda b, t: (mod_row(b), 0, 0)),
            _const_spec((1, D_MODEL)),
            _const_spec((D_MODEL, D_MODEL)),
            _const_spec((D_MODEL, 2 * D_FF)),
            _const_spec((3, 2 * D_FF)),
            _const_spec((D_FF, D_MODEL)),
            _const_spec((1, D_MODEL)),
        ],
        out_specs=pl.BlockSpec((None, rows, D_MODEL), lambda b, t: (b, t, 0)),
        out_shape=jax.ShapeDtypeStruct(h.shape, F32),
        scratch_shapes=[pltpu.VMEM((rows, D_FF), BF16)],
        compiler_params=_params("arbitrary", "arbitrary"),
        name="block_ctx" if context else "block",
    )(h, h, h, attn, attn, attn, gdn, gdn, gdn, hy, hy, hy, mod, gain, w_out, w_up, conv_w, w_down, final_gain)


def _block_diag_ones(width, block):
    i = jnp.arange(width)
    return (i[:, None] // block == i[None, :] // block).astype(BF16)


def _rope_tables():
    n = jnp.arange(SEQ)
    lane = jnp.arange(LANES)
    d = lane % HEAD_DIM
    comp = jnp.where((d < HEAD_DIM // 2)[None, :], (n // GRID_W)[:, None], (n % GRID_W)[:, None]).astype(F32)
    inv = ROPE_THETA ** (-(d % 16).astype(F32) / 16.0)
    ang = comp * inv[None, :]
    sign = jnp.where((d % 32) < 16, -1.0, 1.0)[None, :]
    cos_t = jnp.concatenate([jnp.cos(ang), jnp.ones((CTX_LEN, LANES), F32)], axis=0)
    sin_t = jnp.concatenate([jnp.sin(ang) * sign, jnp.zeros((CTX_LEN, LANES), F32)], axis=0)
    return cos_t, sin_t


def _dft_table(n):
    f = jnp.arange(n, dtype=jnp.int32)
    m = (f[:, None] * f[None, :]) % (2 * n)
    ang = m.astype(F32) * (math.pi / n)
    return jnp.concatenate([jnp.cos(ang), -jnp.sin(ang)], axis=1).astype(BF16)


def _hyena_static(n):
    t = jnp.linspace(0.0, 1.0, n, dtype=F32)[:, None]
    omega = 2.0 * math.pi * jnp.arange(n, dtype=F32)[:, None] / n
    bands = jnp.linspace(1e-4, HYENA_EMB_BANDS - 1, HYENA_EMB_BANDS, dtype=F32)[None, :]
    feats = jnp.concatenate([t, jnp.cos(bands * omega), -jnp.sin(bands * omega)], axis=-1)
    feats = jnp.pad(feats, ((0, 0), (0, LANES - feats.shape[1])))
    max_decay = math.log(HYENA_DECAY_TARGET) / HYENA_FAST_DECAY
    min_decay = math.log(HYENA_DECAY_TARGET) / HYENA_SLOW_DECAY
    deltas = jnp.abs(jnp.linspace(min_decay, max_decay, HYENA_WIDTH, dtype=F32))
    window = jnp.exp(-t * deltas[None, :])
    return feats, window


def kernel(x, c, ctx, c_ctx, w_ada, b_ada, norm1, norm2, w_in, w_out, q_gain, k_gain, gdn_conv, gdn_a_log,
           gdn_dt_bias, gdn_norm, hyena_conv, hyena_w1, hyena_b1, hyena_w2, hyena_b2, hyena_w3, hyena_freq,
           hyena_bias, ffn_up, ffn_conv, ffn_down, final_norm):
    assert x.shape == (BATCH, SEQ, D_MODEL) and ctx.shape == (BATCH, CTX_LEN, D_MODEL)

    h_lat, h_ctx = x, ctx
    cvec = jnp.concatenate([c, c_ctx[None, :], jnp.zeros((MOD_ROWS - BATCH - 1, D_MODEL), F32)], axis=0)
    mod_all = _ada_call(cvec, w_ada, b_ada).reshape(DEPTH, MOD_ROWS, 6, D_MODEL)

    cos_t, sin_t = _rope_tables()
    seg_head = _block_diag_ones(LANES, HEAD_DIM)
    seg_gdn = _block_diag_ones(GDN_WIDTH, GDN_DK)
    ii = jnp.arange(GDN_GROUP * GDN_CHUNK)
    same_chunk = (ii[:, None] // GDN_CHUNK) == (ii[None, :] // GDN_CHUNK)
    tril = jnp.logical_and(same_chunk, ii[:, None] >= ii[None, :]).astype(BF16)
    triu = jnp.logical_and(same_chunk, ii[:, None] <= ii[None, :]).astype(BF16)
    chunk_ones = same_chunk.astype(BF16)
    col = jnp.arange(2 * GDN_WIDTH) // GDN_DK
    expand_g = (jnp.arange(LANES)[:, None] == 8 + col[None, :]).astype(BF16)
    expand_b = (jnp.arange(LANES)[:, None] == col[None, :]).astype(BF16)
    fcat_lat = _dft_table(SEQ)
    fcat_ctx = _dft_table(CTX_LEN)
    feats_lat, win_lat = _hyena_static(SEQ)
    feats_ctx, win_ctx = _hyena_static(CTX_LEN)

    for i in range(DEPTH):
        with_ctx = i < DEPTH - 1
        mod = mod_all[i]
        wi = w_in[i]
        w_cat = jnp.concatenate(
            [wi[:, 0:768], wi[:, 768:1536], wi[:, 1536:1792], wi[:, 1808:2576], wi[:, 1792:1808],
             jnp.zeros((D_MODEL, LANES - 16), F32)], axis=1).astype(BF16)
        q_pad, kt, v, gqkv, ggate, hyp, bd = _in_call(
            h_lat, h_ctx, mod, norm1[i][None, :], w_cat, cos_t, sin_t,
            jnp.tile(q_gain[i], 2)[None, :], jnp.tile(k_gain[i], 2)[None, :], seg_head)

        attn = _attn_call(q_pad, kt, v, with_ctx)

        pad8 = jnp.zeros((8,), F32)
        alog_row = jnp.concatenate([pad8, gdn_a_log[i].reshape(-1), jnp.zeros((LANES - 16,), F32)])[None, :]
        dtb_row = jnp.concatenate([pad8, gdn_dt_bias[i].reshape(-1), jnp.zeros((LANES - 16,), F32)])[None, :]
        gdn = _gdn_call(gqkv, ggate, bd, gdn_conv[i], alog_row, dtb_row, jnp.tile(gdn_norm[i], GDN_HEADS)[None, :],
                        seg_gdn, tril, triu, chunk_ones, expand_g, expand_b)

        w1 = jnp.pad(hyena_w1[i], ((0, LANES - hyena_w1.shape[1]), (0, 0))).astype(BF16)
        filt_args = (w1, hyena_b1[i][None, :], hyena_w2[i].astype(BF16), hyena_b2[i][None, :],
                     hyena_w3[i].astype(BF16), hyena_freq[i][None, :])
        hp, hq, hr = _hyena_filter_call(SEQ, feats_lat, *filt_args, win_lat, fcat_lat)
        hy_lat = _hyena_call(SEQ, 0, hyp, hyena_conv[i], hyena_bias[i], fcat_lat, hp, hq, hr)
        if with_ctx:
            hp, hq, hr = _hyena_filter_call(CTX_LEN, feats_ctx, *filt_args, win_ctx, fcat_ctx)
            hy_ctx = _hyena_call(CTX_LEN, CTX_TILE, hyp, hyena_conv[i], hyena_bias[i], fcat_ctx, hp, hq, hr)

        weights = (mod, norm2[i][None, :], w_out[i].astype(BF16), ffn_up[i].astype(BF16), ffn_conv[i],
                   ffn_down[i].astype(BF16), final_norm[None, :])
        h_lat = _block_call(h_lat, attn, gdn, hy_lat, *weights, rows=FFN_ROWS, context=False, final=not with_ctx)
        if with_ctx:
            h_ctx = _block_call(h_ctx, attn, gdn, hy_ctx, *weights, rows=CTX_LEN, context=True, final=False)

    return h_lat
```

```python
import functools
import math

import jax
import jax.numpy as jnp
from jax import lax
from jax.experimental import pallas as pl
from jax.experimental.pallas import tpu as pltpu

F32 = jnp.float32
BF16 = jnp.bfloat16

D_MODEL = 1024
BATCH = 16
SEQ = 2048
DEPTH = 4
CTX_LEN = 256
STREAM = SEQ + CTX_LEN
TILE = 256
N_TILES = STREAM // TILE
N_LAT_TILES = SEQ // TILE
CTX_TILE = N_LAT_TILES
GRID_W = 64
NORM_EPS = 1e-6
LOG2_E = math.log2(math.e)
HEAD_DIM = 64
LANES = 128
MOD_ROWS = 24
CTX_MOD_ROW = BATCH

ATTN_WIDTH = 512
ATTN_HEADS = 8
ATTN_KV_WIDTH = 128
ROPE_THETA = 10000.0

GDN_WIDTH = 256
GDN_HEADS = 4
GDN_DK = 64
GDN_QKV_WIDTH = 768
GDN_CHUNK = 64
N_CHUNKS = STREAM // GDN_CHUNK
N_LAT_CHUNKS = SEQ // GDN_CHUNK
N_CTX_CHUNKS = CTX_LEN // GDN_CHUNK
GDN_GROUP = 4

HYENA_WIDTH = 256
HYENA_EMB_BANDS = 16
HYENA_FILTER_HIDDEN = 64
HYENA_FAST_DECAY = 0.3
HYENA_SLOW_DECAY = 1.5
HYENA_DECAY_TARGET = 1e-2
HYENA_ROW_BLOCK = 512

D_FF = 2816
FF_CHUNK = 256
N_FF_CHUNKS = D_FF // FF_CHUNK
HALO = 8
FFN_ROWS = 512

COL_ATTN = (0, 768)
COL_GQKV = (768, 1536)
COL_GGATE = (1536, 1792)
COL_HY = (1792, 2560)
COL_BD = (2560, 2688)
IN_COLS = 2688

VMEM_LIMIT = 56 * 1024 * 1024


def _dot(a, b):
    return jnp.dot(a, b, preferred_element_type=F32)


def _dot_nt(a, b):
    return lax.dot_general(a, b, (((1,), (1,)), ((), ())), preferred_element_type=F32)


def _dot_tn(a, b):
    return lax.dot_general(a, b, (((0,), (0,)), ((), ())), preferred_element_type=F32)


def _sigmoid(x):
    return 1.0 / (1.0 + jnp.exp(-x))


def _silu(x):
    return x * _sigmoid(x)


def _softplus(x):
    return jnp.maximum(x, 0.0) + jnp.log(1.0 + jnp.exp(-jnp.abs(x)))


def _seg_sum(x, seg):
    hi = x.astype(BF16)
    lo = (x - hi.astype(F32)).astype(BF16)
    return _dot(hi, seg) + _dot(lo, seg)


def _tri_dot(tri, x):
    x1 = x.astype(BF16)
    r1 = x - x1.astype(F32)
    x2 = r1.astype(BF16)
    x3 = (r1 - x2.astype(F32)).astype(BF16)
    return _dot(tri, x1) + _dot(tri, x2) + _dot(tri, x3)


def _dot_split(x, sel, terms):
    out = None
    for _ in range(terms):
        piece = x.astype(BF16)
        x = x - piece.astype(F32)
        out = _dot(piece, sel) if out is None else out + _dot(piece, sel)
    return out


def _params(*sem):
    return pltpu.CompilerParams(dimension_semantics=sem, vmem_limit_bytes=VMEM_LIMIT)


def _const_spec(shape):
    nd = len(shape)
    return pl.BlockSpec(shape, lambda *_: (0,) * nd, pipeline_mode=pl.Buffered(1))


def _mod_row(b, t):
    return jnp.where(t == CTX_TILE, CTX_MOD_ROW, b)


def _ada_kernel(c_ref, w_ref, b_ref, o_ref):
    s = _silu(c_ref[...]).astype(BF16)
    o_ref[...] = _dot(s, w_ref[...].astype(BF16)) + b_ref[...]


def _ada_call(cvec, w_ada, b_ada):
    nj = 6
    return pl.pallas_call(
        _ada_kernel,
        grid=(DEPTH, nj),
        in_specs=[
            pl.BlockSpec((MOD_ROWS, D_MODEL), lambda i, j: (0, 0)),
            pl.BlockSpec((None, D_MODEL, D_MODEL), lambda i, j: (i, 0, j)),
            pl.BlockSpec((None, 1, D_MODEL), lambda i, j: (i, 0, j)),
        ],
        out_specs=pl.BlockSpec((None, MOD_ROWS, D_MODEL), lambda i, j: (i, 0, j)),
        out_shape=jax.ShapeDtypeStruct((DEPTH, MOD_ROWS, 6 * D_MODEL), F32),
        compiler_params=_params("arbitrary", "arbitrary"),
        name="adaln",
    )(cvec, w_ada, b_ada.reshape(DEPTH, 1, 6 * D_MODEL))


def _norm_mod(x, gain, shift, scale):
    ms = jnp.mean(x * x, axis=-1, keepdims=True)
    return (x * lax.rsqrt(ms + NORM_EPS) * gain) * (1.0 + scale) + shift


def _in_kernel(xl_ref, xc_ref, m_ref, g_ref, w_ref, cos_ref, sin_ref, qg_ref, kg_ref, seg_ref,
               q_ref, kt_ref, v_ref, ogq_ref, ogg_ref, ohy_ref, obd_ref):
    x = jnp.where(pl.program_id(1) == CTX_TILE, xc_ref[...], xl_ref[...])
    u = _norm_mod(x, g_ref[...], m_ref[0:1, :], m_ref[1:2, :]).astype(BF16)
    ogq_ref[...] = _dot(u, w_ref[:, COL_GQKV[0]:COL_GQKV[1]]).astype(BF16)
    ogg_ref[...] = _dot(u, w_ref[:, COL_GGATE[0]:COL_GGATE[1]]).astype(BF16)
    ohy_ref[...] = _dot(u, w_ref[:, COL_HY[0]:COL_HY[1]]).astype(BF16)
    obd_ref[...] = _dot(u, w_ref[:, COL_BD[0]:COL_BD[1]])

    a = _dot(u, w_ref[:, COL_ATTN[0]:COL_ATTN[1]])
    lane = lax.broadcasted_iota(jnp.int32, (TILE, LANES), 1)
    first = (lane % 32) < 16
    low = lane < HEAD_DIM
    seg = seg_ref[...]
    cosv = cos_ref[...]
    sinv = sin_ref[...]

    def norm_rope(x, gain):
        ms = _seg_sum(x * x, seg) * (1.0 / HEAD_DIM)
        y = x * lax.rsqrt(ms + NORM_EPS) * gain
        partner = jnp.where(first, pltpu.roll(y, LANES - 16, 1), pltpu.roll(y, 16, 1))
        return y * cosv + partner * sinv

    for j in range(ATTN_HEADS // 2):
        r = norm_rope(a[:, j * LANES:(j + 1) * LANES], qg_ref[...]) * (HEAD_DIM ** -0.5 * LOG2_E)
        swapped = pltpu.roll(r, HEAD_DIM, 1)
        if j // 2 == 0:
            qa = jnp.where(low, r, 0.0)
            qb = jnp.where(low, swapped, 0.0)
        else:
            qa = jnp.where(low, 0.0, swapped)
            qb = jnp.where(low, 0.0, r)
        q_ref[:, (2 * j) * LANES:(2 * j + 1) * LANES] = qa.astype(BF16)
        q_ref[:, (2 * j + 1) * LANES:(2 * j + 2) * LANES] = qb.astype(BF16)
    rk = norm_rope(a[:, ATTN_WIDTH:ATTN_WIDTH + ATTN_KV_WIDTH], kg_ref[...])
    kt_ref[...] = rk.T.astype(BF16)
    v_ref[...] = a[:, ATTN_WIDTH + ATTN_KV_WIDTH:].astype(BF16)


def _in_call(h_lat, h_ctx, mod, gain, w_cat, cos_t, sin_t, qg, kg, seg):
    def tile(width):
        return pl.BlockSpec((None, TILE, width), lambda b, t: (b, t, 0))
    return pl.pallas_call(
        _in_kernel,
        grid=(BATCH, N_TILES),
        in_specs=[
            pl.BlockSpec((None, TILE, D_MODEL), lambda b, t: (b, jnp.minimum(t, N_LAT_TILES - 1), 0)),
            pl.BlockSpec((None, TILE, D_MODEL), lambda b, t: (b, 0, 0)),
            pl.BlockSpec((None, 6, D_MODEL), lambda b, t: (_mod_row(b, t), 0, 0)),
            _const_spec((1, D_MODEL)),
            _const_spec((D_MODEL, IN_COLS)),
            pl.BlockSpec((TILE, LANES), lambda b, t: (t, 0)),
            pl.BlockSpec((TILE, LANES), lambda b, t: (t, 0)),
            _const_spec((1, LANES)),
            _const_spec((1, LANES)),
            _const_spec((LANES, LANES)),
        ],
        out_specs=[
            tile(ATTN_HEADS * LANES),
            pl.BlockSpec((None, LANES, TILE), lambda b, t: (b, 0, t)),
            tile(LANES), tile(768), tile(256), tile(768), tile(LANES)],
        out_shape=[
            jax.ShapeDtypeStruct((BATCH, STREAM, ATTN_HEADS * LANES), BF16),
            jax.ShapeDtypeStruct((BATCH, LANES, STREAM), BF16),
            jax.ShapeDtypeStruct((BATCH, STREAM, LANES), BF16),
            jax.ShapeDtypeStruct((BATCH, STREAM, 768), BF16),
            jax.ShapeDtypeStruct((BATCH, STREAM, 256), BF16),
            jax.ShapeDtypeStruct((BATCH, STREAM, 768), BF16),
            jax.ShapeDtypeStruct((BATCH, STREAM, LANES), F32),
        ],
        compiler_params=_params("arbitrary", "arbitrary"),
        name="in_proj",
    )(h_lat, h_ctx, mod, gain, w_cat, cos_t, sin_t, qg, kg, seg)


def _attn_kernel(q_ref, kt_ref, v_ref, o_ref, *, with_ctx):
    low = lax.broadcasted_iota(jnp.int32, (TILE, LANES), 1) < HEAD_DIM

    def run(k0, k1):
        kt = kt_ref[:, k0:k1]
        v = v_ref[k0:k1, :]
        s_next = _dot(q_ref[:, 0:LANES], kt)
        outs = []
        for hq in range(ATTN_HEADS):
            s = s_next
            if hq + 1 < ATTN_HEADS:
                s_next = _dot(q_ref[:, (hq + 1) * LANES:(hq + 2) * LANES], kt)
            p = jnp.exp2(s - jnp.max(s, axis=-1, keepdims=True))
            den = jnp.sum(p, axis=-1, keepdims=True)
            outs.append(_dot(p.astype(BF16), v) / den)
            if hq % 2 == 1:
                j = hq // 2
                if j // 2 == 0:
                    blk = jnp.where(low, outs[0], pltpu.roll(outs[1], HEAD_DIM, 1))
                else:
                    blk = jnp.where(low, pltpu.roll(outs[0], HEAD_DIM, 1), outs[1])
                o_ref[:, j * LANES:(j + 1) * LANES] = blk.astype(BF16)
                outs = []

    if with_ctx:
        t = pl.program_id(1)

        @pl.when(t < CTX_TILE)
        def _():
            run(0, STREAM)

        @pl.when(t == CTX_TILE)
        def _():
            run(SEQ, STREAM)
    else:
        run(0, STREAM)


def _attn_call(q_pad, kt, v, with_ctx):
    nt = N_TILES if with_ctx else N_LAT_TILES
    return pl.pallas_call(
        functools.partial(_attn_kernel, with_ctx=with_ctx),
        grid=(BATCH, nt),
        in_specs=[
            pl.BlockSpec((None, TILE, ATTN_HEADS * LANES), lambda b, t: (b, t, 0)),
            pl.BlockSpec((None, LANES, STREAM), lambda b, t: (b, 0, 0)),
            pl.BlockSpec((None, STREAM, LANES), lambda b, t: (b, 0, 0)),
        ],
        out_specs=pl.BlockSpec((None, TILE, ATTN_WIDTH), lambda b, t: (b, t, 0)),
        out_shape=jax.ShapeDtypeStruct((BATCH, STREAM, ATTN_WIDTH), BF16),
        compiler_params=_params("arbitrary", "arbitrary"),
        name="attention",
    )(q_pad, kt, v)


def _gdn_kernel(gq_ref, gg_ref, bd_ref, cw_ref, alog_ref, dtb_ref, ng_ref, seg_ref, tril_ref, triu_ref, ones_ref,
                eg_ref, eb_ref, o_ref, q_s, k_s, v_s, g_s, b_s, u_s, w_s, qk_s, qd_s, kd_s, ge_s, st_s, o_s):
    C = GDN_CHUNK
    seg = seg_ref[...]

    def prep_tile(i, carry):
        r0 = pl.multiple_of(i * TILE, TILE)
        x = gq_ref[pl.ds(r0, TILE), :].astype(F32)
        p0 = pl.multiple_of(jnp.maximum(r0 - 16, 0), 16)
        n0 = pl.multiple_of(jnp.minimum(r0 + TILE, STREAM - 16), 16)
        first_of_segment = jnp.logical_or(i == 0, i == CTX_TILE)
        last_of_segment = jnp.logical_or(i == N_LAT_TILES - 1, i == CTX_TILE)
        xp = gq_ref[pl.ds(p0, 16), :].astype(F32)[15:16, :] * jnp.where(first_of_segment, 0.0, 1.0)
        xn = gq_ref[pl.ds(n0, 16), :].astype(F32)[0:1, :] * jnp.where(last_of_segment, 0.0, 1.0)
        row = lax.broadcasted_iota(jnp.int32, (TILE, GDN_QKV_WIDTH), 0)
        xdn = jnp.where(row == 0, xp, pltpu.roll(x, 1, 0))
        xup = jnp.where(row == TILE - 1, xn, pltpu.roll(x, TILE - 1, 0))
        c = _silu(cw_ref[0:1, :] * xdn + cw_ref[1:2, :] * x + cw_ref[2:3, :] * xup)
        q = c[:, 0:256]
        k = c[:, 256:512]
        q = q * lax.rsqrt(_seg_sum(q * q, seg) + NORM_EPS)
        k = k * lax.rsqrt(_seg_sum(k * k, seg) + NORM_EPS)
        q_s[pl.ds(r0, TILE), :] = q * (GDN_DK ** -0.5)
        k_s[pl.ds(r0, TILE), :] = k
        v_s[pl.ds(r0, TILE), :] = c[:, 512:768]
        bd = bd_ref[pl.ds(r0, TILE), :]
        b_s[pl.ds(r0, TILE), :] = _sigmoid(bd)
        g_s[pl.ds(r0, TILE), :] = -jnp.exp(alog_ref[...]) * _softplus(bd + dtb_ref[...])
        return carry

    lax.fori_loop(0, N_TILES, prep_tile, 0)

    G = GDN_GROUP
    R = G * C
    ri = lax.broadcasted_iota(jnp.int32, (C, GDN_WIDTH), 0)
    cj = lax.broadcasted_iota(jnp.int32, (C, GDN_WIDTH), 1) % C
    lane = lax.broadcasted_iota(jnp.int32, (R, LANES), 1)
    eye4 = jnp.where(ri == cj, 1.0, 0.0)
    same16 = (ri // 16) == (cj // 16)
    off32 = jnp.logical_and((ri // 32) == (cj // 32), jnp.logical_not(same16))
    off64 = (ri // 32) != (cj // 32)
    incl = (ri >= cj, ri <= cj)
    strict = (ri > cj, ri < cj)

    def block_diag(z):
        zb = z.astype(BF16)
        return jnp.concatenate([zb, zb, zb, zb], axis=0) * seg

    def phase1(it, carry):
        c0 = it * G
        r0 = pl.multiple_of(c0 * C, R)
        kc = k_s[pl.ds(r0, R), :]
        qc = q_s[pl.ds(r0, R), :]
        vc = v_s[pl.ds(r0, R), :]
        gt = g_s[pl.ds(r0, R), :]
        bt = b_s[pl.ds(r0, R), :]
        g_fwd = _tri_dot(tril_ref[...], gt)
        g_bwd = _tri_dot(triu_ref[...], gt)
        gc = jnp.where(lane < 8 + GDN_HEADS, g_fwd, g_bwd)
        gx = _dot_split(gc, eg_ref[...], 3)
        bx = _dot_split(bt, eb_ref[...], 2)
        gtx = jnp.concatenate([
            jnp.concatenate([jnp.broadcast_to(gx[g * C + C - 1:g * C + C, 0:GDN_WIDTH], (C, GDN_WIDTH)),
                             jnp.broadcast_to(gx[g * C:g * C + 1, GDN_WIDTH:], (C, GDN_WIDTH))], axis=1)
            for g in range(G)], axis=0)
        e_g = jnp.exp(gx)
        e_gk = jnp.exp(gtx - gx)
        eye_rows = jnp.concatenate([jnp.concatenate([eye4, eye4], axis=1)] * G, axis=0)
        grow = _tri_dot(ones_ref[...], jnp.where(eye_rows > 0.0, gx, 0.0))
        kb = kc.astype(BF16)
        qb = qc.astype(BF16)
        kk, qk0 = [], []
        for g in range(G):
            rs = slice(g * C, (g + 1) * C)
            bdk = jnp.concatenate([kb[rs]] * 4, axis=0) * seg
            kk.append(_dot_nt(kb[rs], bdk))
            qk0.append(_dot_nt(qb[rs], bdk))
        probs = [(g, d) for g in range(G) for d in range(2)]

        def part(arr, g, d):
            return arr[g * C:(g + 1) * C, d * GDN_WIDTH:(d + 1) * GDN_WIDTH]

        decay = [jnp.where(incl[d], jnp.exp(part(gx, g, d) - part(grow, g, d)), 0.0) for g, d in probs]
        a = [jnp.where(strict[d], part(bx, g, d) * kk[g] * dec, 0.0) for (g, d), dec in zip(probs, decay)]
        x = [jnp.where(same16, -ai, 0.0) for ai in a]
        inv = [eye4 + xi for xi in x]
        y = [_dot(xi.astype(BF16), block_diag(xi)) for xi in x]
        for stage in range(3):
            if stage < 2:
                prod = [_dot(yi.astype(BF16), jnp.concatenate([block_diag(ii), block_diag(yi)], axis=1))
                        for yi, ii in zip(y, inv)]
                inv = [ii + pi[:, 0:GDN_WIDTH] for ii, pi in zip(inv, prod)]
                y = [pi[:, GDN_WIDTH:] for pi in prod]
            else:
                inv = [ii + _dot(yi.astype(BF16), block_diag(ii)) for yi, ii in zip(y, inv)]
        for off in (off32, off64):
            t = [_dot(ii.astype(BF16), block_diag(jnp.where(off, ai, 0.0))) for ii, ai in zip(inv, a)]
            inv = [ii - _dot(ti.astype(BF16), block_diag(ii)) for ii, ti in zip(inv, t)]
        for (g, d), ii, dec in zip(probs, inv, decay):
            rs = slice(g * C, (g + 1) * C)
            bxi = part(bx, g, d)
            egi = part(e_g, g, d)
            rhs = jnp.concatenate([block_diag(vc[rs] * bxi), block_diag(kc[rs] * (bxi * egi))], axis=1)
            r = _dot(ii.astype(BF16), rhs)
            u_s[c0 + g, d] = r[:, 0:GDN_WIDTH]
            w_s[c0 + g, d] = r[:, GDN_WIDTH:].astype(BF16)
            qk_s[c0 + g, d] = (qk0[g] * dec).astype(BF16)
            qd_s[c0 + g, d] = (qc[rs] * egi).astype(BF16)
            kd_s[c0 + g, d] = (kc[rs] * part(e_gk, g, d)).astype(BF16)
            ge_s[c0 + g, d] = jnp.exp(part(gtx, g, d)[0:8, :])
        return carry

    lax.fori_loop(0, N_CHUNKS // G, phase1, 0)

    st_s[...] = jnp.zeros(st_s.shape, F32)
    o_s[...] = jnp.zeros(o_s.shape, F32)
    bi = lax.broadcasted_iota(jnp.int32, (GDN_WIDTH, GDN_WIDTH), 0) // C
    bj = lax.broadcasted_iota(jnp.int32, (GDN_WIDTH, GDN_WIDTH), 1) // C
    on_diag = bi == bj

    def phase2(s, carry):
        cs = (jnp.where(s < N_CTX_CHUNKS, s + N_LAT_CHUNKS, s - N_CTX_CHUNKS), N_CHUNKS - 1 - s)
        st = [st_s[d] for d in range(2)]
        stb = [x.astype(BF16) for x in st]
        r1 = [_dot(jnp.concatenate([w_s[cs[d], d], qd_s[cs[d], d]], axis=0), stb[d]) for d in range(2)]
        vn = [u_s[cs[d], d] - r1[d][0:C, :] for d in range(2)]
        vnb = [x.astype(BF16) for x in vn]
        o = [r1[d][C:2 * C, :] + _dot(qk_s[cs[d], d], jnp.concatenate([vnb[d]] * 4, axis=0) * seg)
             for d in range(2)]
        upd = [_dot_tn(kd_s[cs[d], d], vnb[d]) for d in range(2)]
        for d in range(2):
            st_s[d] = st[d] * ge_s[cs[d], d][0:1, :] + jnp.where(on_diag, upd[d], 0.0)
            r0 = pl.multiple_of(cs[d] * C, C)
            o_s[pl.ds(r0, C), :] += o[d]
        return carry

    lax.fori_loop(0, N_CHUNKS, phase2, 0)

    def finish_tile(i, carry):
        r0 = pl.multiple_of(i * TILE, TILE)
        o = o_s[pl.ds(r0, TILE), :]
        ms = _seg_sum(o * o, seg) * (1.0 / GDN_DK)
        y = o * lax.rsqrt(ms + NORM_EPS) * ng_ref[...]
        o_ref[pl.ds(r0, TILE), :] = (y * _silu(gg_ref[pl.ds(r0, TILE), :].astype(F32))).astype(BF16)
        return carry

    lax.fori_loop(0, N_TILES, finish_tile, 0)


def _gdn_call(gqkv, ggate, bd, conv_w, alog_row, dtb_row, ng_row, seg, tril, triu, ones, expand_g, expand_b):
    def per_batch(width):
        return pl.BlockSpec((None, STREAM, width), lambda b: (b, 0, 0))
    rows = GDN_GROUP * GDN_CHUNK
    packed = (N_CHUNKS, 2, GDN_CHUNK, GDN_WIDTH)
    return pl.pallas_call(
        _gdn_kernel,
        grid=(BATCH,),
        in_specs=[
            per_batch(768), per_batch(256), per_batch(LANES),
            _const_spec((3, 768)), _const_spec((1, LANES)), _const_spec((1, LANES)), _const_spec((1, GDN_WIDTH)),
            _const_spec((GDN_WIDTH, GDN_WIDTH)), _const_spec((rows, rows)), _const_spec((rows, rows)),
            _const_spec((rows, rows)), _const_spec((LANES, 2 * GDN_WIDTH)), _const_spec((LANES, 2 * GDN_WIDTH)),
        ],
        out_specs=pl.BlockSpec((None, STREAM, GDN_WIDTH), lambda b: (b, 0, 0)),
        out_shape=jax.ShapeDtypeStruct((BATCH, STREAM, GDN_WIDTH), BF16),
        scratch_shapes=[
            pltpu.VMEM((STREAM, GDN_WIDTH), F32),
            pltpu.VMEM((STREAM, GDN_WIDTH), F32),
            pltpu.VMEM((STREAM, GDN_WIDTH), F32),
            pltpu.VMEM((STREAM, LANES), F32),
            pltpu.VMEM((STREAM, LANES), F32),
            pltpu.VMEM(packed, F32),
            pltpu.VMEM(packed, BF16),
            pltpu.VMEM(packed, BF16),
            pltpu.VMEM(packed, BF16),
            pltpu.VMEM(packed, BF16),
            pltpu.VMEM((N_CHUNKS, 2, 8, GDN_WIDTH), F32),
            pltpu.VMEM((2, GDN_WIDTH, GDN_WIDTH), F32),
            pltpu.VMEM((STREAM, GDN_WIDTH), F32),
        ],
        compiler_params=_params("arbitrary"),
        name="gdn",
    )(gqkv, ggate, bd, conv_w, alog_row, dtb_row, ng_row, seg, tril, triu, ones, expand_g, expand_b)


def _hyena_filter_kernel(feat_ref, w1_ref, b1_ref, w2_ref, b2_ref, w3_ref, fr_ref, win_ref, fc_ref,
                         p_ref, q_ref, r_ref, *, n):
    fr = fr_ref[...]
    hid = jnp.sin(fr * (_dot(feat_ref[...].astype(BF16), w1_ref[...]) + b1_ref[...]))
    hid = jnp.sin(fr * (_dot(hid.astype(BF16), w2_ref[...]) + b2_ref[...]))
    filt = _dot(hid.astype(BF16), w3_ref[...])
    win = win_ref[...]
    row = lax.broadcasted_iota(jnp.int32, (n, HYENA_WIDTH), 0)
    h_fwd = filt[:, 0:HYENA_WIDTH] * win
    h_bwd = jnp.where(row == 0, 0.0, filt[:, HYENA_WIDTH:] * win)
    hcat = jnp.concatenate([h_fwd, h_bwd], axis=1).astype(BF16)
    h_re = _dot(fc_ref[:, 0:n], hcat)
    h_im = _dot(fc_ref[:, n:2 * n], hcat)
    sgn = jnp.where((row & 1) == 0, 1.0, -1.0)
    nyq = jnp.sum(sgn * (h_fwd + h_bwd), axis=0, keepdims=True)
    re = h_re[:, 0:HYENA_WIDTH] + h_re[:, HYENA_WIDTH:]
    p_ref[...] = re
    q_ref[...] = h_im[:, 0:HYENA_WIDTH] - h_im[:, HYENA_WIDTH:]
    r_ref[...] = jnp.where(row == 0, nyq, re)


def _hyena_filter_call(n, feats, w1, b1, w2, b2, w3, fr, win, fcat):
    shapes = [(n, LANES), (LANES, HYENA_FILTER_HIDDEN), (1, HYENA_FILTER_HIDDEN),
              (HYENA_FILTER_HIDDEN, HYENA_FILTER_HIDDEN), (1, HYENA_FILTER_HIDDEN),
              (HYENA_FILTER_HIDDEN, 2 * HYENA_WIDTH), (1, HYENA_FILTER_HIDDEN), (n, HYENA_WIDTH), (n, 2 * n)]
    out = jax.ShapeDtypeStruct((n, HYENA_WIDTH), F32)
    return pl.pallas_call(
        functools.partial(_hyena_filter_kernel, n=n),
        grid=(1,),
        in_specs=[_const_spec(s) for s in shapes],
        out_specs=[pl.BlockSpec((n, HYENA_WIDTH), lambda i: (0, 0))] * 3,
        out_shape=[out, out, out],
        compiler_params=_params("arbitrary"),
        name=f"hyena_filter_{n}",
    )(feats, w1, b1, w2, b2, w3, fr, win, fcat)


def _hyena_kernel(x_ref, cw_ref, bias_ref, fc_ref, p_ref, q_ref, r_ref, o_ref, w_s, x0_s, z_s, *, n):
    rows = min(n, TILE)
    n_tiles = n // rows
    inv_n = 1.0 / (2 * n)

    def conv_tile(i, carry):
        r0 = pl.multiple_of(i * rows, rows)
        x = x_ref[pl.ds(r0, rows), :].astype(F32)
        p0 = pl.multiple_of(jnp.maximum(r0 - 16, 0), 16)
        n0 = pl.multiple_of(jnp.minimum(r0 + rows, n - 16), 16)
        xp = x_ref[pl.ds(p0, 16), :].astype(F32)[15:16, :] * jnp.where(i == 0, 0.0, 1.0)
        xn = x_ref[pl.ds(n0, 16), :].astype(F32)[0:1, :] * jnp.where(i == n_tiles - 1, 0.0, 1.0)
        row = lax.broadcasted_iota(jnp.int32, (rows, 3 * HYENA_WIDTH), 0)
        xdn = jnp.where(row == 0, xp, pltpu.roll(x, 1, 0))
        xup = jnp.where(row == rows - 1, xn, pltpu.roll(x, rows - 1, 0))
        c = cw_ref[0:1, :] * xdn + cw_ref[1:2, :] * x + cw_ref[2:3, :] * xup
        x0_s[pl.ds(r0, rows), :] = c[:, 0:HYENA_WIDTH]
        w_s[pl.ds(r0, rows), :] = c[:, 2 * HYENA_WIDTH:] * c[:, HYENA_WIDTH:2 * HYENA_WIDTH]
        return carry

    lax.fori_loop(0, n_tiles, conv_tile, 0)

    wb = w_s[...].astype(BF16)
    rb = min(n, HYENA_ROW_BLOCK)
    row = lax.broadcasted_iota(jnp.int32, (rb, HYENA_WIDTH), 0)
    sgn = jnp.where((row & 1) == 0, 1.0, -1.0)
    u_nyq = jnp.sum(sgn * w_s[0:rb, :], axis=0, keepdims=True)
    for r0 in range(rb, n, rb):
        u_nyq = u_nyq + jnp.sum(sgn * w_s[r0:r0 + rb, :], axis=0, keepdims=True)
    z_nyq = None
    for r0 in range(0, n, rb):
        rows = slice(r0, r0 + rb)
        u_re = _dot(fc_ref[rows, 0:n], wb)
        u_im = _dot(fc_ref[rows, n:2 * n], wb)
        fscale = 2.0 * inv_n
        if r0 == 0:
            u_im = jnp.where(row == 0, u_nyq, u_im)
            fscale = jnp.where(row == 0, inv_n, 2.0 * inv_n)
        z_re = u_re * p_ref[rows, :] - u_im * q_ref[rows, :]
        z_im = u_re * q_ref[rows, :] + u_im * r_ref[rows, :]
        if r0 == 0:
            z_nyq = z_im[0:1, :] * inv_n
        z_s[r0:r0 + rb, :] = (z_re * fscale).astype(BF16)
        z_s[n + r0:n + r0 + rb, :] = (z_im * fscale).astype(BF16)
    for r0 in range(0, n, rb):
        rows = slice(r0, r0 + rb)
        y = _dot(fc_ref[rows, :], z_s[...]) + sgn * z_nyq + w_s[rows, :] * bias_ref[...]
        o_ref[rows, :] = (y * x0_s[rows, :]).astype(BF16)


def _hyena_call(n, tile_index, hy, conv_w, bias, fcat, hp, hq, hr):
    return pl.pallas_call(
        functools.partial(_hyena_kernel, n=n),
        grid=(BATCH,),
        in_specs=[
            pl.BlockSpec((None, n, 3 * HYENA_WIDTH), lambda b: (b, tile_index, 0)),
            _const_spec((3, 3 * HYENA_WIDTH)),
            _const_spec((1, HYENA_WIDTH)),
            _const_spec((n, 2 * n)),
            _const_spec((n, HYENA_WIDTH)), _const_spec((n, HYENA_WIDTH)), _const_spec((n, HYENA_WIDTH)),
        ],
        out_specs=pl.BlockSpec((None, n, HYENA_WIDTH), lambda b: (b, 0, 0)),
        out_shape=jax.ShapeDtypeStruct((BATCH, n, HYENA_WIDTH), BF16),
        scratch_shapes=[
            pltpu.VMEM((n, HYENA_WIDTH), F32),
            pltpu.VMEM((n, HYENA_WIDTH), F32),
            pltpu.VMEM((2 * n, HYENA_WIDTH), BF16),
        ],
        compiler_params=_params("arbitrary"),
        name=f"hyena_{n}",
    )(hy, conv_w, bias, fcat, hp, hq, hr)


def _block_kernel(h_ref, hp_ref, hn_ref, a_ref, ap_ref, an_ref, d_ref, dp_ref, dn_ref, y_ref, yp_ref, yn_ref,
                  m_ref, g_ref, wo_ref, wu_ref, cw_ref, wd_ref, fg_ref, o_ref, act_s, *, rows, final):
    t = pl.program_id(1)
    ext = rows + 2 * HALO

    def mixed(h, a, d, y):
        acc = _dot(a, wo_ref[0:512, :]) + _dot(d, wo_ref[512:768, :]) + _dot(y, wo_ref[768:1024, :])
        return h + m_ref[2:3, :] * acc

    h1 = mixed(h_ref[...], a_ref[...], d_ref[...], y_ref[...])
    h1_top = mixed(jnp.concatenate([hp_ref[...], hp_ref[...]], axis=0), ap_ref[...], dp_ref[...],
                   yp_ref[...])[HALO:2 * HALO, :]
    h1_bottom = mixed(jnp.concatenate([hn_ref[...], hn_ref[...]], axis=0), an_ref[...], dn_ref[...],
                      yn_ref[...])[0:HALO, :]
    gain, shift, scale = g_ref[...], m_ref[3:4, :], m_ref[4:5, :]
    u = jnp.concatenate([
        jnp.where(t == 0, 0.0, _norm_mod(h1_top, gain, shift, scale)).astype(BF16),
        _norm_mod(h1, gain, shift, scale).astype(BF16),
        jnp.where(t == pl.num_programs(1) - 1, 0.0, _norm_mod(h1_bottom, gain, shift, scale)).astype(BF16)],
        axis=0)
    def up_conv(col):
        cols = slice(col, col + FF_CHUNK)
        p = _dot(u, wu_ref[:, cols])
        c = (cw_ref[0:1, cols] * pltpu.roll(p, 1, 0) + cw_ref[1:2, cols] * p
             + cw_ref[2:3, cols] * pltpu.roll(p, ext - 1, 0))
        return c[HALO:HALO + rows, :]

    for j in range(N_FF_CHUNKS):
        gate = up_conv(j * FF_CHUNK)
        value = up_conv(D_FF + j * FF_CHUNK)
        act_s[:, j * FF_CHUNK:(j + 1) * FF_CHUNK] = (_silu(gate) * value).astype(BF16)
    res = h1 + m_ref[5:6, :] * _dot(act_s[...], wd_ref[...])
    if final:
        ms = jnp.mean(res * res, axis=-1, keepdims=True)
        res = res * lax.rsqrt(ms + NORM_EPS) * fg_ref[...]
    o_ref[...] = res


def _block_call(h, attn, gdn, hy, mod, gain, w_out, w_up, conv_w, w_down, final_gain, *, rows, context, final):
    seg_rows = h.shape[1]
    base = SEQ if context else 0
    wide = 2 * HALO
    mod_row = (lambda b: CTX_MOD_ROW) if context else (lambda b: b)

    def main(width, offset):
        return pl.BlockSpec((None, rows, width), lambda b, t: (b, offset // rows + t, 0))

    def halos(width, block, offset, total):
        per, first, last = rows // block, offset // block, total // block - 1
        return [pl.BlockSpec((None, block, width), lambda b, t: (b, jnp.maximum(first + t * per - 1, 0), 0)),
                pl.BlockSpec((None, block, width), lambda b, t: (b, jnp.minimum(first + (t + 1) * per, last), 0))]

    return pl.pallas_call(
        functools.partial(_block_kernel, rows=rows, final=final),
        grid=(BATCH, seg_rows // rows),
        in_specs=[
            main(D_MODEL, 0), *halos(D_MODEL, HALO, 0, seg_rows),
            main(ATTN_WIDTH, base), *halos(ATTN_WIDTH, wide, base, STREAM),
            main(GDN_WIDTH, base), *halos(GDN_WIDTH, wide, base, STREAM),
            main(HYENA_WIDTH, 0), *halos(HYENA_WIDTH, wide, 0, seg_rows),
            pl.BlockSpec((None, 6, D_MODEL), lambda b, t: (mod_row(b), 0, 0)),
            _const_spec((1, D_MODEL)),
            _const_spec((D_MODEL, D_MODEL)),
            _const_spec((D_MODEL, 2 * D_FF)),
            _const_spec((3, 2 * D_FF)),
            _const_spec((D_FF, D_MODEL)),
            _const_spec((1, D_MODEL)),
        ],
        out_specs=pl.BlockSpec((None, rows, D_MODEL), lambda b, t: (b, t, 0)),
        out_shape=jax.ShapeDtypeStruct(h.shape, F32),
        scratch_shapes=[pltpu.VMEM((rows, D_FF), BF16)],
        compiler_params=_params("arbitrary", "arbitrary"),
        name="block_ctx" if context else "block",
    )(h, h, h, attn, attn, attn, gdn, gdn, gdn, hy, hy, hy, mod, gain, w_out, w_up, conv_w, w_down, final_gain)


def _block_diag_ones(width, block):
    i = jnp.arange(width)
    return (i[:, None] // block == i[None, :] // block).astype(BF16)


def _rope_tables():
    n = jnp.arange(SEQ)
    lane = jnp.arange(LANES)
    d = lane % HEAD_DIM
    comp = jnp.where((d < HEAD_DIM // 2)[None, :], (n // GRID_W)[:, None], (n % GRID_W)[:, None]).astype(F32)
    inv = ROPE_THETA ** (-(d % 16).astype(F32) / 16.0)
    ang = comp * inv[None, :]
    sign = jnp.where((d % 32) < 16, -1.0, 1.0)[None, :]
    cos_t = jnp.concatenate([jnp.cos(ang), jnp.ones((CTX_LEN, LANES), F32)], axis=0)
    sin_t = jnp.concatenate([jnp.sin(ang) * sign, jnp.zeros((CTX_LEN, LANES), F32)], axis=0)
    return cos_t, sin_t


def _dft_table(n):
    f = jnp.arange(n, dtype=jnp.int32)
    m = (f[:, None] * f[None, :]) % (2 * n)
    ang = m.astype(F32) * (math.pi / n)
    return jnp.concatenate([jnp.cos(ang), -jnp.sin(ang)], axis=1).astype(BF16)


def _hyena_static(n):
    t = jnp.linspace(0.0, 1.0, n, dtype=F32)[:, None]
    omega = 2.0 * math.pi * jnp.arange(n, dtype=F32)[:, None] / n
    bands = jnp.linspace(1e-4, HYENA_EMB_BANDS - 1, HYENA_EMB_BANDS, dtype=F32)[None, :]
    feats = jnp.concatenate([t, jnp.cos(bands * omega), -jnp.sin(bands * omega)], axis=-1)
    feats = jnp.pad(feats, ((0, 0), (0, LANES - feats.shape[1])))
    max_decay = math.log(HYENA_DECAY_TARGET) / HYENA_FAST_DECAY
    min_decay = math.log(HYENA_DECAY_TARGET) / HYENA_SLOW_DECAY
    deltas = jnp.abs(jnp.linspace(min_decay, max_decay, HYENA_WIDTH, dtype=F32))
    window = jnp.exp(-t * deltas[None, :])
    return feats, window


def kernel(x, c, ctx, c_ctx, w_ada, b_ada, norm1, norm2, w_in, w_out, q_gain, k_gain, gdn_conv, gdn_a_log,
           gdn_dt_bias, gdn_norm, hyena_conv, hyena_w1, hyena_b1, hyena_w2, hyena_b2, hyena_w3, hyena_freq,
           hyena_bias, ffn_up, ffn_conv, ffn_down, final_norm):
    assert x.shape == (BATCH, SEQ, D_MODEL) and ctx.shape == (BATCH, CTX_LEN, D_MODEL)

    h_lat, h_ctx = x, ctx
    cvec = jnp.concatenate([c, c_ctx[None, :], jnp.zeros((MOD_ROWS - BATCH - 1, D_MODEL), F32)], axis=0)
    mod_all = _ada_call(cvec, w_ada, b_ada).reshape(DEPTH, MOD_ROWS, 6, D_MODEL)

    cos_t, sin_t = _rope_tables()
    seg_head = _block_diag_ones(LANES, HEAD_DIM)
    seg_gdn = _block_diag_ones(GDN_WIDTH, GDN_DK)
    ii = jnp.arange(GDN_GROUP * GDN_CHUNK)
    same_chunk = (ii[:, None] // GDN_CHUNK) == (ii[None, :] // GDN_CHUNK)
    tril = jnp.logical_and(same_chunk, ii[:, None] >= ii[None, :]).astype(BF16)
    triu = jnp.logical_and(same_chunk, ii[:, None] <= ii[None, :]).astype(BF16)
    chunk_ones = same_chunk.astype(BF16)
    col = jnp.arange(2 * GDN_WIDTH) // GDN_DK
    expand_g = (jnp.arange(LANES)[:, None] == 8 + col[None, :]).astype(BF16)
    expand_b = (jnp.arange(LANES)[:, None] == col[None, :]).astype(BF16)
    fcat_lat = _dft_table(SEQ)
    fcat_ctx = _dft_table(CTX_LEN)
    feats_lat, win_lat = _hyena_static(SEQ)
    feats_ctx, win_ctx = _hyena_static(CTX_LEN)

    for i in range(DEPTH):
        with_ctx = i < DEPTH - 1
        mod = mod_all[i]
        wi = w_in[i]
        w_cat = jnp.concatenate(
            [wi[:, 0:768], wi[:, 768:1536], wi[:, 1536:1792], wi[:, 1808:2576], wi[:, 1792:1808],
             jnp.zeros((D_MODEL, LANES - 16), F32)], axis=1).astype(BF16)
        q_pad, kt, v, gqkv, ggate, hyp, bd = _in_call(
            h_lat, h_ctx, mod, norm1[i][None, :], w_cat, cos_t, sin_t,
            jnp.tile(q_gain[i], 2)[None, :], jnp.tile(k_gain[i], 2)[None, :], seg_head)

        attn = _attn_call(q_pad, kt, v, with_ctx)

        pad8 = jnp.zeros((8,), F32)
        alog_row = jnp.concatenate([pad8, gdn_a_log[i].reshape(-1), jnp.zeros((LANES - 16,), F32)])[None, :]
        dtb_row = jnp.concatenate([pad8, gdn_dt_bias[i].reshape(-1), jnp.zeros((LANES - 16,), F32)])[None, :]
        gdn = _gdn_call(gqkv, ggate, bd, gdn_conv[i], alog_row, dtb_row, jnp.tile(gdn_norm[i], GDN_HEADS)[None, :],
                        seg_gdn, tril, triu, chunk_ones, expand_g, expand_b)

        w1 = jnp.pad(hyena_w1[i], ((0, LANES - hyena_w1.shape[1]), (0, 0))).astype(BF16)
        filt_args = (w1, hyena_b1[i][None, :], hyena_w2[i].astype(BF16), hyena_b2[i][None, :],
                     hyena_w3[i].astype(BF16), hyena_freq[i][None, :])
        hp, hq, hr = _hyena_filter_call(SEQ, feats_lat, *filt_args, win_lat, fcat_lat)
        hy_lat = _hyena_call(SEQ, 0, hyp, hyena_conv[i], hyena_bias[i], fcat_lat, hp, hq, hr)
        if with_ctx:
            hp, hq, hr = _hyena_filter_call(CTX_LEN, feats_ctx, *filt_args, win_ctx, fcat_ctx)
            hy_ctx = _hyena_call(CTX_LEN, CTX_TILE, hyp, hyena_conv[i], hyena_bias[i], fcat_ctx, hp, hq, hr)

        weights = (mod, norm2[i][None, :], w_out[i].astype(BF16), ffn_up[i].astype(BF16), ffn_conv[i],
                   ffn_down[i].astype(BF16), final_norm[None, :])
        h_lat = _block_call(h_lat, attn, gdn, hy_lat, *weights, rows=FFN_ROWS, context=False, final=not with_ctx)
        if with_ctx:
            h_ctx = _block_call(h_ctx, attn, gdn, hy_ctx, *weights, rows=CTX_LEN, context=True, final=False)

    return h_lat
```

```python
import functools
import math

import jax
import jax.numpy as jnp
from jax import lax
from jax.experimental import pallas as pl
from jax.experimental.pallas import tpu as pltpu

F32 = jnp.float32
BF16 = jnp.bfloat16

D_MODEL = 1024
BATCH = 16
SEQ = 2048
DEPTH = 4
CTX_LEN = 256
STREAM = SEQ + CTX_LEN
TILE = 256
N_TILES = STREAM // TILE
IN_ROWS = 512
STREAM_PAD = (SEQ // IN_ROWS + 1) * IN_ROWS
N_LAT_TILES = SEQ // TILE
CTX_TILE = N_LAT_TILES
GRID_W = 64
NORM_EPS = 1e-6
LOG2_E = math.log2(math.e)
HEAD_DIM = 64
LANES = 128
MOD_ROWS = 24
CTX_MOD_ROW = BATCH

ATTN_WIDTH = 512
ATTN_HEADS = 8
ATTN_KV_WIDTH = 128
ROPE_THETA = 10000.0

GDN_WIDTH = 256
GDN_HEADS = 4
GDN_DK = 64
GDN_QKV_WIDTH = 768
GDN_CHUNK = 64
N_CHUNKS = STREAM // GDN_CHUNK
N_LAT_CHUNKS = SEQ // GDN_CHUNK
N_CTX_CHUNKS = CTX_LEN // GDN_CHUNK
GDN_GROUP = 4

HYENA_WIDTH = 256
HYENA_EMB_BANDS = 16
HYENA_FILTER_HIDDEN = 64
HYENA_FAST_DECAY = 0.3
HYENA_SLOW_DECAY = 1.5
HYENA_DECAY_TARGET = 1e-2
HYENA_ROW_BLOCK = 512

D_FF = 2816
FF_CHUNK = 256
N_FF_CHUNKS = D_FF // FF_CHUNK
HALO = 8
FFN_ROWS = 512

COL_ATTN = (0, 768)
COL_GQKV = (768, 1536)
COL_GGATE = (1536, 1792)
COL_HY = (1792, 2560)
COL_BD = (2560, 2688)
IN_COLS = 2688

VMEM_LIMIT = 56 * 1024 * 1024


def _dot(a, b):
    return jnp.dot(a, b, preferred_element_type=F32)


def _dot_nt(a, b):
    return lax.dot_general(a, b, (((1,), (1,)), ((), ())), preferred_element_type=F32)


def _dot_tn(a, b):
    return lax.dot_general(a, b, (((0,), (0,)), ((), ())), preferred_element_type=F32)


def _sigmoid(x):
    return 1.0 / (1.0 + jnp.exp(-x))


def _silu(x):
    return x * _sigmoid(x)


def _softplus(x):
    return jnp.maximum(x, 0.0) + jnp.log(1.0 + jnp.exp(-jnp.abs(x)))


def _seg_sum(x, seg):
    hi = x.astype(BF16)
    lo = (x - hi.astype(F32)).astype(BF16)
    return _dot(hi, seg) + _dot(lo, seg)


def _tri_dot(tri, x):
    x1 = x.astype(BF16)
    r1 = x - x1.astype(F32)
    x2 = r1.astype(BF16)
    x3 = (r1 - x2.astype(F32)).astype(BF16)
    return _dot(tri, x1) + _dot(tri, x2) + _dot(tri, x3)


def _dot_split(x, sel, terms):
    out = None
    for _ in range(terms):
        piece = x.astype(BF16)
        x = x - piece.astype(F32)
        out = _dot(piece, sel) if out is None else out + _dot(piece, sel)
    return out


def _params(*sem):
    return pltpu.CompilerParams(dimension_semantics=sem, vmem_limit_bytes=VMEM_LIMIT)


def _const_spec(shape):
    nd = len(shape)
    return pl.BlockSpec(shape, lambda *_: (0,) * nd, pipeline_mode=pl.Buffered(1))


def _ada_kernel(c_ref, w_ref, b_ref, o_ref):
    s = _silu(c_ref[...]).astype(BF16)
    o_ref[...] = _dot(s, w_ref[...].astype(BF16)) + b_ref[...]


def _ada_call(cvec, w_ada, b_ada):
    nj = 6
    return pl.pallas_call(
        _ada_kernel,
        grid=(DEPTH, nj),
        in_specs=[
            pl.BlockSpec((MOD_ROWS, D_MODEL), lambda i, j: (0, 0)),
            pl.BlockSpec((None, D_MODEL, D_MODEL), lambda i, j: (i, 0, j)),
            pl.BlockSpec((None, 1, D_MODEL), lambda i, j: (i, 0, j)),
        ],
        out_specs=pl.BlockSpec((None, MOD_ROWS, D_MODEL), lambda i, j: (i, 0, j)),
        out_shape=jax.ShapeDtypeStruct((DEPTH, MOD_ROWS, 6 * D_MODEL), F32),
        compiler_params=_params("arbitrary", "arbitrary"),
        name="adaln",
    )(cvec, w_ada, b_ada.reshape(DEPTH, 1, 6 * D_MODEL))


def _norm_mod(x, gain, shift, scale):
    ms = jnp.mean(x * x, axis=-1, keepdims=True)
    return (x * lax.rsqrt(ms + NORM_EPS) * gain) * (1.0 + scale) + shift


def _in_kernel(xl_ref, xc_ref, m_ref, g_ref, w_ref, cos_ref, sin_ref, qg_ref, kg_ref, seg_ref,
               q_ref, kt_ref, v_ref, ogq_ref, ogg_ref, ohy_ref, obd_ref):
    seg = seg_ref[...]

    def project(x, rows):
        u = _norm_mod(x, g_ref[...], m_ref[0:1, :], m_ref[1:2, :]).astype(BF16)
        ogq_ref[0:rows, :] = _dot(u, w_ref[:, COL_GQKV[0]:COL_GQKV[1]]).astype(BF16)
        ogg_ref[0:rows, :] = _dot(u, w_ref[:, COL_GGATE[0]:COL_GGATE[1]]).astype(BF16)
        ohy_ref[0:rows, :] = _dot(u, w_ref[:, COL_HY[0]:COL_HY[1]]).astype(BF16)
        obd_ref[0:rows, :] = _dot(u, w_ref[:, COL_BD[0]:COL_BD[1]])

        a = _dot(u, w_ref[:, COL_ATTN[0]:COL_ATTN[1]])
        lane = lax.broadcasted_iota(jnp.int32, (rows, LANES), 1)
        first = (lane % 32) < 16
        low = lane < HEAD_DIM
        cosv = cos_ref[0:rows, :]
        sinv = sin_ref[0:rows, :]

        def norm_rope(x, gain):
            ms = _seg_sum(x * x, seg) * (1.0 / HEAD_DIM)
            y = x * lax.rsqrt(ms + NORM_EPS) * gain
            partner = jnp.where(first, pltpu.roll(y, LANES - 16, 1), pltpu.roll(y, 16, 1))
            return y * cosv + partner * sinv

        for j in range(ATTN_HEADS // 2):
            r = norm_rope(a[:, j * LANES:(j + 1) * LANES], qg_ref[...]) * (HEAD_DIM ** -0.5 * LOG2_E)
            swapped = pltpu.roll(r, HEAD_DIM, 1)
            if j // 2 == 0:
                qa = jnp.where(low, r, 0.0)
                qb = jnp.where(low, swapped, 0.0)
            else:
                qa = jnp.where(low, 0.0, swapped)
                qb = jnp.where(low, 0.0, r)
            q_ref[0:rows, (2 * j) * LANES:(2 * j + 1) * LANES] = qa.astype(BF16)
            q_ref[0:rows, (2 * j + 1) * LANES:(2 * j + 2) * LANES] = qb.astype(BF16)
        rk = norm_rope(a[:, ATTN_WIDTH:ATTN_WIDTH + ATTN_KV_WIDTH], kg_ref[...])
        kt_ref[:, 0:rows] = rk.T.astype(BF16)
        v_ref[0:rows, :] = a[:, ATTN_WIDTH + ATTN_KV_WIDTH:].astype(BF16)

    t = pl.program_id(1)

    @pl.when(t < SEQ // IN_ROWS)
    def _():
        project(xl_ref[...], IN_ROWS)

    @pl.when(t == SEQ // IN_ROWS)
    def _():
        project(xc_ref[...], CTX_LEN)
        pad = IN_ROWS - CTX_LEN
        for ref in (q_ref, v_ref, ogq_ref, ogg_ref, ohy_ref, obd_ref):
            ref[CTX_LEN:IN_ROWS, :] = jnp.zeros((pad, ref.shape[1]), ref.dtype)
        kt_ref[:, CTX_LEN:IN_ROWS] = jnp.zeros((LANES, pad), BF16)


def _in_call(h_lat, h_ctx, mod, gain, w_cat, cos_t, sin_t, qg, kg, seg):
    n_lat = SEQ // IN_ROWS

    def tile(width):
        return pl.BlockSpec((None, IN_ROWS, width), lambda b, t: (b, t, 0))

    def padded(width, dtype):
        return jax.ShapeDtypeStruct((BATCH, STREAM_PAD, width), dtype)
    return pl.pallas_call(
        _in_kernel,
        grid=(BATCH, n_lat + 1),
        in_specs=[
            pl.BlockSpec((None, IN_ROWS, D_MODEL), lambda b, t: (b, jnp.minimum(t, n_lat - 1), 0)),
            pl.BlockSpec((None, CTX_LEN, D_MODEL), lambda b, t: (b, 0, 0)),
            pl.BlockSpec((None, 6, D_MODEL), lambda b, t: (jnp.where(t == n_lat, CTX_MOD_ROW, b), 0, 0)),
            _const_spec((1, D_MODEL)),
            _const_spec((D_MODEL, IN_COLS)),
            pl.BlockSpec((IN_ROWS, LANES), lambda b, t: (t, 0)),
            pl.BlockSpec((IN_ROWS, LANES), lambda b, t: (t, 0)),
            _const_spec((1, LANES)),
            _const_spec((1, LANES)),
            _const_spec((LANES, LANES)),
        ],
        out_specs=[
            tile(ATTN_HEADS * LANES),
            pl.BlockSpec((None, LANES, IN_ROWS), lambda b, t: (b, 0, t)),
            tile(LANES), tile(768), tile(256), tile(768), tile(LANES)],
        out_shape=[
            padded(ATTN_HEADS * LANES, BF16),
            jax.ShapeDtypeStruct((BATCH, LANES, STREAM_PAD), BF16),
            padded(LANES, BF16),
            padded(768, BF16), padded(256, BF16), padded(768, BF16), padded(LANES, F32),
        ],
        compiler_params=_params("arbitrary", "arbitrary"),
        name="in_proj",
    )(h_lat, h_ctx, mod, gain, w_cat, cos_t, sin_t, qg, kg, seg)


def _attn_kernel(q_ref, kt_ref, v_ref, o_ref, *, with_ctx):
    low = lax.broadcasted_iota(jnp.int32, (TILE, LANES), 1) < HEAD_DIM

    def run(k0, k1):
        kt = kt_ref[:, k0:k1]
        v = v_ref[k0:k1, :]
        s_next = _dot(q_ref[:, 0:LANES], kt)
        outs = []
        for hq in range(ATTN_HEADS):
            s = s_next
            if hq + 1 < ATTN_HEADS:
                s_next = _dot(q_ref[:, (hq + 1) * LANES:(hq + 2) * LANES], kt)
            p = jnp.exp2(s - jnp.max(s, axis=-1, keepdims=True))
            den = jnp.sum(p, axis=-1, keepdims=True)
            outs.append(_dot(p.astype(BF16), v) / den)
            if hq % 2 == 1:
                j = hq // 2
                if j // 2 == 0:
                    blk = jnp.where(low, outs[0], pltpu.roll(outs[1], HEAD_DIM, 1))
                else:
                    blk = jnp.where(low, pltpu.roll(outs[0], HEAD_DIM, 1), outs[1])
                o_ref[:, j * LANES:(j + 1) * LANES] = blk.astype(BF16)
                outs = []

    if with_ctx:
        t = pl.program_id(1)

        @pl.when(t < CTX_TILE)
        def _():
            run(0, STREAM)

        @pl.when(t == CTX_TILE)
        def _():
            run(SEQ, STREAM)
    else:
        run(0, STREAM)


def _attn_call(q_pad, kt, v, with_ctx):
    nt = N_TILES if with_ctx else N_LAT_TILES
    return pl.pallas_call(
        functools.partial(_attn_kernel, with_ctx=with_ctx),
        grid=(BATCH, nt),
        in_specs=[
            pl.BlockSpec((None, TILE, ATTN_HEADS * LANES), lambda b, t: (b, t, 0)),
            pl.BlockSpec((None, LANES, STREAM), lambda b, t: (b, 0, 0)),
            pl.BlockSpec((None, STREAM, LANES), lambda b, t: (b, 0, 0)),
        ],
        out_specs=pl.BlockSpec((None, TILE, ATTN_WIDTH), lambda b, t: (b, t, 0)),
        out_shape=jax.ShapeDtypeStruct((BATCH, STREAM, ATTN_WIDTH), BF16),
        compiler_params=_params("arbitrary", "arbitrary"),
        name="attention",
    )(q_pad, kt, v)


def _gdn_kernel(gq_ref, gg_ref, bd_ref, cw_ref, alog_ref, dtb_ref, ng_ref, seg_ref, tril_ref, triu_ref, ones_ref,
                eg_ref, eb_ref, o_ref, q_s, k_s, v_s, g_s, b_s, u_s, w_s, qk_s, qd_s, kd_s, ge_s, st_s, o_s):
    C = GDN_CHUNK
    seg = seg_ref[...]

    def prep_tile(i, carry):
        r0 = pl.multiple_of(i * TILE, TILE)
        x = gq_ref[pl.ds(r0, TILE), :].astype(F32)
        p0 = pl.multiple_of(jnp.maximum(r0 - 16, 0), 16)
        n0 = pl.multiple_of(jnp.minimum(r0 + TILE, STREAM - 16), 16)
        first_of_segment = jnp.logical_or(i == 0, i == CTX_TILE)
        last_of_segment = jnp.logical_or(i == N_LAT_TILES - 1, i == CTX_TILE)
        xp = gq_ref[pl.ds(p0, 16), :].astype(F32)[15:16, :] * jnp.where(first_of_segment, 0.0, 1.0)
        xn = gq_ref[pl.ds(n0, 16), :].astype(F32)[0:1, :] * jnp.where(last_of_segment, 0.0, 1.0)
        row = lax.broadcasted_iota(jnp.int32, (TILE, GDN_QKV_WIDTH), 0)
        xdn = jnp.where(row == 0, xp, pltpu.roll(x, 1, 0))
        xup = jnp.where(row == TILE - 1, xn, pltpu.roll(x, TILE - 1, 0))
        c = _silu(cw_ref[0:1, :] * xdn + cw_ref[1:2, :] * x + cw_ref[2:3, :] * xup)
        q = c[:, 0:256]
        k = c[:, 256:512]
        q = q * lax.rsqrt(_seg_sum(q * q, seg) + NORM_EPS)
        k = k * lax.rsqrt(_seg_sum(k * k, seg) + NORM_EPS)
        q_s[pl.ds(r0, TILE), :] = q * (GDN_DK ** -0.5)
        k_s[pl.ds(r0, TILE), :] = k
        v_s[pl.ds(r0, TILE), :] = c[:, 512:768]
        bd = bd_ref[pl.ds(r0, TILE), :]
        b_s[pl.ds(r0, TILE), :] = _sigmoid(bd)
        g_s[pl.ds(r0, TILE), :] = -jnp.exp(alog_ref[...]) * _softplus(bd + dtb_ref[...])
        return carry

    lax.fori_loop(0, N_TILES, prep_tile, 0)

    G = GDN_GROUP
    R = G * C
    ri = lax.broadcasted_iota(jnp.int32, (C, GDN_WIDTH), 0)
    cj = lax.broadcasted_iota(jnp.int32, (C, GDN_WIDTH), 1) % C
    lane = lax.broadcasted_iota(jnp.int32, (R, LANES), 1)
    eye4 = jnp.where(ri == cj, 1.0, 0.0)
    same16 = (ri // 16) == (cj // 16)
    off32 = jnp.logical_and((ri // 32) == (cj // 32), jnp.logical_not(same16))
    off64 = (ri // 32) != (cj // 32)
    incl = (ri >= cj, ri <= cj)
    strict = (ri > cj, ri < cj)

    def block_diag(z):
        zb = z.astype(BF16)
        return jnp.concatenate([zb, zb, zb, zb], axis=0) * seg

    def phase1(it, carry):
        c0 = it * G
        r0 = pl.multiple_of(c0 * C, R)
        kc = k_s[pl.ds(r0, R), :]
        qc = q_s[pl.ds(r0, R), :]
        vc = v_s[pl.ds(r0, R), :]
        gt = g_s[pl.ds(r0, R), :]
        bt = b_s[pl.ds(r0, R), :]
        g_fwd = _tri_dot(tril_ref[...], gt)
        g_bwd = _tri_dot(triu_ref[...], gt)
        gc = jnp.where(lane < 8 + GDN_HEADS, g_fwd, g_bwd)
        gx = _dot_split(gc, eg_ref[...], 3)
        bx = _dot_split(bt, eb_ref[...], 2)
        gtx = jnp.concatenate([
            jnp.concatenate([jnp.broadcast_to(gx[g * C + C - 1:g * C + C, 0:GDN_WIDTH], (C, GDN_WIDTH)),
                             jnp.broadcast_to(gx[g * C:g * C + 1, GDN_WIDTH:], (C, GDN_WIDTH))], axis=1)
            for g in range(G)], axis=0)
        e_g = jnp.exp(gx)
        e_gk = jnp.exp(gtx - gx)
        eye_rows = jnp.concatenate([jnp.concatenate([eye4, eye4], axis=1)] * G, axis=0)
        grow = _tri_dot(ones_ref[...], jnp.where(eye_rows > 0.0, gx, 0.0))
        kb = kc.astype(BF16)
        qb = qc.astype(BF16)
        kk, qk0 = [], []
        for g in range(G):
            rs = slice(g * C, (g + 1) * C)
            bdk = jnp.concatenate([kb[rs]] * 4, axis=0) * seg
            kk.append(_dot_nt(kb[rs], bdk))
            qk0.append(_dot_nt(qb[rs], bdk))
        probs = [(g, d) for g in range(G) for d in range(2)]

        def part(arr, g, d):
            return arr[g * C:(g + 1) * C, d * GDN_WIDTH:(d + 1) * GDN_WIDTH]

        decay = [jnp.where(incl[d], jnp.exp(part(gx, g, d) - part(grow, g, d)), 0.0) for g, d in probs]
        a = [jnp.where(strict[d], part(bx, g, d) * kk[g] * dec, 0.0) for (g, d), dec in zip(probs, decay)]
        x = [jnp.where(same16, -ai, 0.0) for ai in a]
        inv = [eye4 + xi for xi in x]
        y = [_dot(xi.astype(BF16), block_diag(xi)) for xi in x]
        for stage in range(3):
            if stage < 2:
                prod = [_dot(yi.astype(BF16), jnp.concatenate([block_diag(ii), block_diag(yi)], axis=1))
                        for yi, ii in zip(y, inv)]
                inv = [ii + pi[:, 0:GDN_WIDTH] for ii, pi in zip(inv, prod)]
                y = [pi[:, GDN_WIDTH:] for pi in prod]
            else:
                inv = [ii + _dot(yi.astype(BF16), block_diag(ii)) for yi, ii in zip(y, inv)]
        for off in (off32, off64):
            t = [_dot(ii.astype(BF16), block_diag(jnp.where(off, ai, 0.0))) for ii, ai in zip(inv, a)]
            inv = [ii - _dot(ti.astype(BF16), block_diag(ii)) for ii, ti in zip(inv, t)]
        for (g, d), ii, dec in zip(probs, inv, decay):
            rs = slice(g * C, (g + 1) * C)
            bxi = part(bx, g, d)
            egi = part(e_g, g, d)
            rhs = jnp.concatenate([block_diag(vc[rs] * bxi), block_diag(kc[rs] * (bxi * egi))], axis=1)
            r = _dot(ii.astype(BF16), rhs)
            u_s[c0 + g, d] = r[:, 0:GDN_WIDTH]
            w_s[c0 + g, d] = r[:, GDN_WIDTH:].astype(BF16)
            qk_s[c0 + g, d] = (qk0[g] * dec).astype(BF16)
            qd_s[c0 + g, d] = (qc[rs] * egi).astype(BF16)
            kd_s[c0 + g, d] = (kc[rs] * part(e_gk, g, d)).astype(BF16)
            ge_s[c0 + g, d] = jnp.exp(part(gtx, g, d)[0:8, :])
        return carry

    lax.fori_loop(0, N_CHUNKS // G, phase1, 0)

    st_s[...] = jnp.zeros(st_s.shape, F32)
    o_s[...] = jnp.zeros(o_s.shape, F32)
    bi = lax.broadcasted_iota(jnp.int32, (GDN_WIDTH, GDN_WIDTH), 0) // C
    bj = lax.broadcasted_iota(jnp.int32, (GDN_WIDTH, GDN_WIDTH), 1) // C
    on_diag = bi == bj

    def phase2(s, carry):
        cs = (jnp.where(s < N_CTX_CHUNKS, s + N_LAT_CHUNKS, s - N_CTX_CHUNKS), N_CHUNKS - 1 - s)
        st = [st_s[d] for d in range(2)]
        stb = [x.astype(BF16) for x in st]
        r1 = [_dot(jnp.concatenate([w_s[cs[d], d], qd_s[cs[d], d]], axis=0), stb[d]) for d in range(2)]
        vn = [u_s[cs[d], d] - r1[d][0:C, :] for d in range(2)]
        vnb = [x.astype(BF16) for x in vn]
        o = [r1[d][C:2 * C, :] + _dot(qk_s[cs[d], d], jnp.concatenate([vnb[d]] * 4, axis=0) * seg)
             for d in range(2)]
        upd = [_dot_tn(kd_s[cs[d], d], vnb[d]) for d in range(2)]
        for d in range(2):
            st_s[d] = st[d] * ge_s[cs[d], d][0:1, :] + jnp.where(on_diag, upd[d], 0.0)
            r0 = pl.multiple_of(cs[d] * C, C)
            o_s[pl.ds(r0, C), :] += o[d]
        return carry

    lax.fori_loop(0, N_CHUNKS, phase2, 0)

    def finish_tile(i, carry):
        r0 = pl.multiple_of(i * TILE, TILE)
        o = o_s[pl.ds(r0, TILE), :]
        ms = _seg_sum(o * o, seg) * (1.0 / GDN_DK)
        y = o * lax.rsqrt(ms + NORM_EPS) * ng_ref[...]
        o_ref[pl.ds(r0, TILE), :] = (y * _silu(gg_ref[pl.ds(r0, TILE), :].astype(F32))).astype(BF16)
        return carry

    lax.fori_loop(0, N_TILES, finish_tile, 0)


def _gdn_call(gqkv, ggate, bd, conv_w, alog_row, dtb_row, ng_row, seg, tril, triu, ones, expand_g, expand_b):
    def per_batch(width):
        return pl.BlockSpec((None, STREAM, width), lambda b: (b, 0, 0))
    rows = GDN_GROUP * GDN_CHUNK
    packed = (N_CHUNKS, 2, GDN_CHUNK, GDN_WIDTH)
    return pl.pallas_call(
        _gdn_kernel,
        grid=(BATCH,),
        in_specs=[
            per_batch(768), per_batch(256), per_batch(LANES),
            _const_spec((3, 768)), _const_spec((1, LANES)), _const_spec((1, LANES)), _const_spec((1, GDN_WIDTH)),
            _const_spec((GDN_WIDTH, GDN_WIDTH)), _const_spec((rows, rows)), _const_spec((rows, rows)),
            _const_spec((rows, rows)), _const_spec((LANES, 2 * GDN_WIDTH)), _const_spec((LANES, 2 * GDN_WIDTH)),
        ],
        out_specs=pl.BlockSpec((None, STREAM, GDN_WIDTH), lambda b: (b, 0, 0)),
        out_shape=jax.ShapeDtypeStruct((BATCH, STREAM, GDN_WIDTH), BF16),
        scratch_shapes=[
            pltpu.VMEM((STREAM, GDN_WIDTH), F32),
            pltpu.VMEM((STREAM, GDN_WIDTH), F32),
            pltpu.VMEM((STREAM, GDN_WIDTH), F32),
            pltpu.VMEM((STREAM, LANES), F32),
            pltpu.VMEM((STREAM, LANES), F32),
            pltpu.VMEM(packed, F32),
            pltpu.VMEM(packed, BF16),
            pltpu.VMEM(packed, BF16),
            pltpu.VMEM(packed, BF16),
            pltpu.VMEM(packed, BF16),
            pltpu.VMEM((N_CHUNKS, 2, 8, GDN_WIDTH), F32),
            pltpu.VMEM((2, GDN_WIDTH, GDN_WIDTH), F32),
            pltpu.VMEM((STREAM, GDN_WIDTH), F32),
        ],
        compiler_params=_params("arbitrary"),
        name="gdn",
    )(gqkv, ggate, bd, conv_w, alog_row, dtb_row, ng_row, seg, tril, triu, ones, expand_g, expand_b)


def _hyena_filter_kernel(feat_ref, w1_ref, b1_ref, w2_ref, b2_ref, w3_ref, fr_ref, win_ref, fc_ref,
                         p_ref, q_ref, r_ref, *, n):
    fr = fr_ref[...]
    hid = jnp.sin(fr * (_dot(feat_ref[...].astype(BF16), w1_ref[...]) + b1_ref[...]))
    hid = jnp.sin(fr * (_dot(hid.astype(BF16), w2_ref[...]) + b2_ref[...]))
    filt = _dot(hid.astype(BF16), w3_ref[...])
    win = win_ref[...]
    row = lax.broadcasted_iota(jnp.int32, (n, HYENA_WIDTH), 0)
    h_fwd = filt[:, 0:HYENA_WIDTH] * win
    h_bwd = jnp.where(row == 0, 0.0, filt[:, HYENA_WIDTH:] * win)
    hcat = jnp.concatenate([h_fwd, h_bwd], axis=1).astype(BF16)
    h_re = _dot(fc_ref[:, 0:n], hcat)
    h_im = _dot(fc_ref[:, n:2 * n], hcat)
    sgn = jnp.where((row & 1) == 0, 1.0, -1.0)
    nyq = jnp.sum(sgn * (h_fwd + h_bwd), axis=0, keepdims=True)
    re = h_re[:, 0:HYENA_WIDTH] + h_re[:, HYENA_WIDTH:]
    p_ref[...] = re
    q_ref[...] = h_im[:, 0:HYENA_WIDTH] - h_im[:, HYENA_WIDTH:]
    r_ref[...] = jnp.where(row == 0, nyq, re)


def _hyena_filter_call(n, feats, w1, b1, w2, b2, w3, fr, win, fcat):
    shapes = [(n, LANES), (LANES, HYENA_FILTER_HIDDEN), (1, HYENA_FILTER_HIDDEN),
              (HYENA_FILTER_HIDDEN, HYENA_FILTER_HIDDEN), (1, HYENA_FILTER_HIDDEN),
              (HYENA_FILTER_HIDDEN, 2 * HYENA_WIDTH), (1, HYENA_FILTER_HIDDEN), (n, HYENA_WIDTH), (n, 2 * n)]
    out = jax.ShapeDtypeStruct((n, HYENA_WIDTH), F32)
    return pl.pallas_call(
        functools.partial(_hyena_filter_kernel, n=n),
        grid=(1,),
        in_specs=[_const_spec(s) for s in shapes],
        out_specs=[pl.BlockSpec((n, HYENA_WIDTH), lambda i: (0, 0))] * 3,
        out_shape=[out, out, out],
        compiler_params=_params("arbitrary"),
        name=f"hyena_filter_{n}",
    )(feats, w1, b1, w2, b2, w3, fr, win, fcat)


def _hyena_kernel(x_ref, cw_ref, bias_ref, fc_ref, p_ref, q_ref, r_ref, o_ref, w_s, x0_s, z_s, *, n):
    rows = min(n, TILE)
    n_tiles = n // rows
    inv_n = 1.0 / (2 * n)

    def conv_tile(i, carry):
        r0 = pl.multiple_of(i * rows, rows)
        x = x_ref[pl.ds(r0, rows), :].astype(F32)
        p0 = pl.multiple_of(jnp.maximum(r0 - 16, 0), 16)
        n0 = pl.multiple_of(jnp.minimum(r0 + rows, n - 16), 16)
        xp = x_ref[pl.ds(p0, 16), :].astype(F32)[15:16, :] * jnp.where(i == 0, 0.0, 1.0)
        xn = x_ref[pl.ds(n0, 16), :].astype(F32)[0:1, :] * jnp.where(i == n_tiles - 1, 0.0, 1.0)
        row = lax.broadcasted_iota(jnp.int32, (rows, 3 * HYENA_WIDTH), 0)
        xdn = jnp.where(row == 0, xp, pltpu.roll(x, 1, 0))
        xup = jnp.where(row == rows - 1, xn, pltpu.roll(x, rows - 1, 0))
        c = cw_ref[0:1, :] * xdn + cw_ref[1:2, :] * x + cw_ref[2:3, :] * xup
        x0_s[pl.ds(r0, rows), :] = c[:, 0:HYENA_WIDTH]
        w_s[pl.ds(r0, rows), :] = c[:, 2 * HYENA_WIDTH:] * c[:, HYENA_WIDTH:2 * HYENA_WIDTH]
        return carry

    lax.fori_loop(0, n_tiles, conv_tile, 0)

    wb = w_s[...].astype(BF16)
    rb = min(n, HYENA_ROW_BLOCK)
    row = lax.broadcasted_iota(jnp.int32, (rb, HYENA_WIDTH), 0)
    sgn = jnp.where((row & 1) == 0, 1.0, -1.0)
    u_nyq = jnp.sum(sgn * w_s[0:rb, :], axis=0, keepdims=True)
    for r0 in range(rb, n, rb):
        u_nyq = u_nyq + jnp.sum(sgn * w_s[r0:r0 + rb, :], axis=0, keepdims=True)
    z_nyq = None
    for r0 in range(0, n, rb):
        rows = slice(r0, r0 + rb)
        u_re = _dot(fc_ref[rows, 0:n], wb)
        u_im = _dot(fc_ref[rows, n:2 * n], wb)
        fscale = 2.0 * inv_n
        if r0 == 0:
            u_im = jnp.where(row == 0, u_nyq, u_im)
            fscale = jnp.where(row == 0, inv_n, 2.0 * inv_n)
        z_re = u_re * p_ref[rows, :] - u_im * q_ref[rows, :]
        z_im = u_re * q_ref[rows, :] + u_im * r_ref[rows, :]
        if r0 == 0:
            z_nyq = z_im[0:1, :] * inv_n
        z_s[r0:r0 + rb, :] = (z_re * fscale).astype(BF16)
        z_s[n + r0:n + r0 + rb, :] = (z_im * fscale).astype(BF16)
    for r0 in range(0, n, rb):
        rows = slice(r0, r0 + rb)
        y = _dot(fc_ref[rows, :], z_s[...]) + sgn * z_nyq + w_s[rows, :] * bias_ref[...]
        o_ref[rows, :] = (y * x0_s[rows, :]).astype(BF16)


def _hyena_call(n, tile_index, hy, conv_w, bias, fcat, hp, hq, hr):
    return pl.pallas_call(
        functools.partial(_hyena_kernel, n=n),
        grid=(BATCH,),
        in_specs=[
            pl.BlockSpec((None, n, 3 * HYENA_WIDTH), lambda b: (b, tile_index, 0)),
            _const_spec((3, 3 * HYENA_WIDTH)),
            _const_spec((1, HYENA_WIDTH)),
            _const_spec((n, 2 * n)),
            _const_spec((n, HYENA_WIDTH)), _const_spec((n, HYENA_WIDTH)), _const_spec((n, HYENA_WIDTH)),
        ],
        out_specs=pl.BlockSpec((None, n, HYENA_WIDTH), lambda b: (b, 0, 0)),
        out_shape=jax.ShapeDtypeStruct((BATCH, n, HYENA_WIDTH), BF16),
        scratch_shapes=[
            pltpu.VMEM((n, HYENA_WIDTH), F32),
            pltpu.VMEM((n, HYENA_WIDTH), F32),
            pltpu.VMEM((2 * n, HYENA_WIDTH), BF16),
        ],
        compiler_params=_params("arbitrary"),
        name=f"hyena_{n}",
    )(hy, conv_w, bias, fcat, hp, hq, hr)


def _block_kernel(h_ref, hp_ref, hn_ref, a_ref, ap_ref, an_ref, d_ref, dp_ref, dn_ref, y_ref, yp_ref, yn_ref,
                  m_ref, g_ref, wo_ref, wu_ref, cw_ref, wd_ref, fg_ref, o_ref, act_s, *, rows, final):
    t = pl.program_id(1)
    ext = rows + 2 * HALO

    def mixed(h, a, d, y):
        acc = _dot(a, wo_ref[0:512, :]) + _dot(d, wo_ref[512:768, :]) + _dot(y, wo_ref[768:1024, :])
        return h + m_ref[2:3, :] * acc

    h1 = mixed(h_ref[...], a_ref[...], d_ref[...], y_ref[...])
    h1_top = mixed(jnp.concatenate([hp_ref[...], hp_ref[...]], axis=0), ap_ref[...], dp_ref[...],
                   yp_ref[...])[HALO:2 * HALO, :]
    h1_bottom = mixed(jnp.concatenate([hn_ref[...], hn_ref[...]], axis=0), an_ref[...], dn_ref[...],
                      yn_ref[...])[0:HALO, :]
    gain, shift, scale = g_ref[...], m_ref[3:4, :], m_ref[4:5, :]
    u = jnp.concatenate([
        jnp.where(t == 0, 0.0, _norm_mod(h1_top, gain, shift, scale)).astype(BF16),
        _norm_mod(h1, gain, shift, scale).astype(BF16),
        jnp.where(t == pl.num_programs(1) - 1, 0.0, _norm_mod(h1_bottom, gain, shift, scale)).astype(BF16)],
        axis=0)
    def up_conv(col):
        cols = slice(col, col + FF_CHUNK)
        p = _dot(u, wu_ref[:, cols])
        c = (cw_ref[0:1, cols] * pltpu.roll(p, 1, 0) + cw_ref[1:2, cols] * p
             + cw_ref[2:3, cols] * pltpu.roll(p, ext - 1, 0))
        return c[HALO:HALO + rows, :]

    for j in range(N_FF_CHUNKS):
        gate = up_conv(j * FF_CHUNK)
        value = up_conv(D_FF + j * FF_CHUNK)
        act_s[:, j * FF_CHUNK:(j + 1) * FF_CHUNK] = (_silu(gate) * value).astype(BF16)
    res = h1 + m_ref[5:6, :] * _dot(act_s[...], wd_ref[...])
    if final:
        ms = jnp.mean(res * res, axis=-1, keepdims=True)
        res = res * lax.rsqrt(ms + NORM_EPS) * fg_ref[...]
    o_ref[...] = res


def _block_call(h, attn, gdn, hy, mod, gain, w_out, w_up, conv_w, w_down, final_gain, *, rows, context, final):
    seg_rows = h.shape[1]
    base = SEQ if context else 0
    wide = 2 * HALO
    mod_row = (lambda b: CTX_MOD_ROW) if context else (lambda b: b)

    def main(width, offset):
        return pl.BlockSpec((None, rows, width), lambda b, t: (b, offset // rows + t, 0))

    def halos(width, block, offset, total):
        per, first, last = rows // block, offset // block, total // block - 1
        return [pl.BlockSpec((None, block, width), lambda b, t: (b, jnp.maximum(first + t * per - 1, 0), 0)),
                pl.BlockSpec((None, block, width), lambda b, t: (b, jnp.minimum(first + (t + 1) * per, last), 0))]

    return pl.pallas_call(
        functools.partial(_block_kernel, rows=rows, final=final),
        grid=(BATCH, seg_rows // rows),
        in_specs=[
            main(D_MODEL, 0), *halos(D_MODEL, HALO, 0, seg_rows),
            main(ATTN_WIDTH, base), *halos(ATTN_WIDTH, wide, base, STREAM),
            main(GDN_WIDTH, base), *halos(GDN_WIDTH, wide, base, STREAM),
            main(HYENA_WIDTH, 0), *halos(HYENA_WIDTH, wide, 0, seg_rows),
            pl.BlockSpec((None, 6, D_MODEL), lambda b, t: (mod_row(b), 0, 0)),
            _const_spec((1, D_MODEL)),
            _const_spec((D_MODEL, D_MODEL)),
            _const_spec((D_MODEL, 2 * D_FF)),
            _const_spec((3, 2 * D_FF)),
            _const_spec((D_FF, D_MODEL)),
            _const_spec((1, D_MODEL)),
        ],
        out_specs=pl.BlockSpec((None, rows, D_MODEL), lambda b, t: (b, t, 0)),
        out_shape=jax.ShapeDtypeStruct(h.shape, F32),
        scratch_shapes=[pltpu.VMEM((rows, D_FF), BF16)],
        compiler_params=_params("arbitrary", "arbitrary"),
        name="block_ctx" if context else "block",
    )(h, h, h, attn, attn, attn, gdn, gdn, gdn, hy, hy, hy, mod, gain, w_out, w_up, conv_w, w_down, final_gain)


def _block_diag_ones(width, block):
    i = jnp.arange(width)
    return (i[:, None] // block == i[None, :] // block).astype(BF16)


def _rope_tables():
    n = jnp.arange(SEQ)
    lane = jnp.arange(LANES)
    d = lane % HEAD_DIM
    comp = jnp.where((d < HEAD_DIM // 2)[None, :], (n // GRID_W)[:, None], (n % GRID_W)[:, None]).astype(F32)
    inv = ROPE_THETA ** (-(d % 16).astype(F32) / 16.0)
    ang = comp * inv[None, :]
    sign = jnp.where((d % 32) < 16, -1.0, 1.0)[None, :]
    rest = STREAM_PAD - SEQ
    cos_t = jnp.concatenate([jnp.cos(ang), jnp.ones((rest, LANES), F32)], axis=0)
    sin_t = jnp.concatenate([jnp.sin(ang) * sign, jnp.zeros((rest, LANES), F32)], axis=0)
    return cos_t, sin_t


def _dft_table(n):
    f = jnp.arange(n, dtype=jnp.int32)
    m = (f[:, None] * f[None, :]) % (2 * n)
    ang = m.astype(F32) * (math.pi / n)
    return jnp.concatenate([jnp.cos(ang), -jnp.sin(ang)], axis=1).astype(BF16)


def _hyena_static(n):
    t = jnp.linspace(0.0, 1.0, n, dtype=F32)[:, None]
    omega = 2.0 * math.pi * jnp.arange(n, dtype=F32)[:, None] / n
    bands = jnp.linspace(1e-4, HYENA_EMB_BANDS - 1, HYENA_EMB_BANDS, dtype=F32)[None, :]
    feats = jnp.concatenate([t, jnp.cos(bands * omega), -jnp.sin(bands * omega)], axis=-1)
    feats = jnp.pad(feats, ((0, 0), (0, LANES - feats.shape[1])))
    max_decay = math.log(HYENA_DECAY_TARGET) / HYENA_FAST_DECAY
    min_decay = math.log(HYENA_DECAY_TARGET) / HYENA_SLOW_DECAY
    deltas = jnp.abs(jnp.linspace(min_decay, max_decay, HYENA_WIDTH, dtype=F32))
    window = jnp.exp(-t * deltas[None, :])
    return feats, window


def kernel(x, c, ctx, c_ctx, w_ada, b_ada, norm1, norm2, w_in, w_out, q_gain, k_gain, gdn_conv, gdn_a_log,
           gdn_dt_bias, gdn_norm, hyena_conv, hyena_w1, hyena_b1, hyena_w2, hyena_b2, hyena_w3, hyena_freq,
           hyena_bias, ffn_up, ffn_conv, ffn_down, final_norm):
    assert x.shape == (BATCH, SEQ, D_MODEL) and ctx.shape == (BATCH, CTX_LEN, D_MODEL)

    h_lat, h_ctx = x, ctx
    cvec = jnp.concatenate([c, c_ctx[None, :], jnp.zeros((MOD_ROWS - BATCH - 1, D_MODEL), F32)], axis=0)
    mod_all = _ada_call(cvec, w_ada, b_ada).reshape(DEPTH, MOD_ROWS, 6, D_MODEL)

    cos_t, sin_t = _rope_tables()
    seg_head = _block_diag_ones(LANES, HEAD_DIM)
    seg_gdn = _block_diag_ones(GDN_WIDTH, GDN_DK)
    ii = jnp.arange(GDN_GROUP * GDN_CHUNK)
    same_chunk = (ii[:, None] // GDN_CHUNK) == (ii[None, :] // GDN_CHUNK)
    tril = jnp.logical_and(same_chunk, ii[:, None] >= ii[None, :]).astype(BF16)
    triu = jnp.logical_and(same_chunk, ii[:, None] <= ii[None, :]).astype(BF16)
    chunk_ones = same_chunk.astype(BF16)
    col = jnp.arange(2 * GDN_WIDTH) // GDN_DK
    expand_g = (jnp.arange(LANES)[:, None] == 8 + col[None, :]).astype(BF16)
    expand_b = (jnp.arange(LANES)[:, None] == col[None, :]).astype(BF16)
    fcat_lat = _dft_table(SEQ)
    fcat_ctx = _dft_table(CTX_LEN)
    feats_lat, win_lat = _hyena_static(SEQ)
    feats_ctx, win_ctx = _hyena_static(CTX_LEN)

    for i in range(DEPTH):
        with_ctx = i < DEPTH - 1
        mod = mod_all[i]
        wi = w_in[i]
        w_cat = jnp.concatenate(
            [wi[:, 0:768], wi[:, 768:1536], wi[:, 1536:1792], wi[:, 1808:2576], wi[:, 1792:1808],
             jnp.zeros((D_MODEL, LANES - 16), F32)], axis=1).astype(BF16)
        q_pad, kt, v, gqkv, ggate, hyp, bd = _in_call(
            h_lat, h_ctx, mod, norm1[i][None, :], w_cat, cos_t, sin_t,
            jnp.tile(q_gain[i], 2)[None, :], jnp.tile(k_gain[i], 2)[None, :], seg_head)

        attn = _attn_call(q_pad, kt, v, with_ctx)

        pad8 = jnp.zeros((8,), F32)
        alog_row = jnp.concatenate([pad8, gdn_a_log[i].reshape(-1), jnp.zeros((LANES - 16,), F32)])[None, :]
        dtb_row = jnp.concatenate([pad8, gdn_dt_bias[i].reshape(-1), jnp.zeros((LANES - 16,), F32)])[None, :]
        gdn = _gdn_call(gqkv, ggate, bd, gdn_conv[i], alog_row, dtb_row, jnp.tile(gdn_norm[i], GDN_HEADS)[None, :],
                        seg_gdn, tril, triu, chunk_ones, expand_g, expand_b)

        w1 = jnp.pad(hyena_w1[i], ((0, LANES - hyena_w1.shape[1]), (0, 0))).astype(BF16)
        filt_args = (w1, hyena_b1[i][None, :], hyena_w2[i].astype(BF16), hyena_b2[i][None, :],
                     hyena_w3[i].astype(BF16), hyena_freq[i][None, :])
        hp, hq, hr = _hyena_filter_call(SEQ, feats_lat, *filt_args, win_lat, fcat_lat)
        hy_lat = _hyena_call(SEQ, 0, hyp, hyena_conv[i], hyena_bias[i], fcat_lat, hp, hq, hr)
        if with_ctx:
            hp, hq, hr = _hyena_filter_call(CTX_LEN, feats_ctx, *filt_args, win_ctx, fcat_ctx)
            hy_ctx = _hyena_call(CTX_LEN, CTX_TILE, hyp, hyena_conv[i], hyena_bias[i], fcat_ctx, hp, hq, hr)

        weights = (mod, norm2[i][None, :], w_out[i].astype(BF16), ffn_up[i].astype(BF16), ffn_conv[i],
                   ffn_down[i].astype(BF16), final_norm[None, :])
        h_lat = _block_call(h_lat, attn, gdn, hy_lat, *weights, rows=FFN_ROWS, context=False, final=not with_ctx)
        if with_ctx:
            h_ctx = _block_call(h_ctx, attn, gdn, hy_ctx, *weights, rows=CTX_LEN, context=True, final=False)

    return h_lat
```

```python
import functools
import math

import jax
import jax.numpy as jnp
from jax import lax
from jax.experimental import pallas as pl
from jax.experimental.pallas import tpu as pltpu

F32 = jnp.float32
BF16 = jnp.bfloat16

D_MODEL = 1024
BATCH = 16
SEQ = 2048
DEPTH = 4
CTX_LEN = 256
STREAM = SEQ + CTX_LEN
TILE = 256
N_TILES = STREAM // TILE
IN_ROWS = 512
STREAM_PAD = (SEQ // IN_ROWS + 1) * IN_ROWS
N_LAT_TILES = SEQ // TILE
CTX_TILE = N_LAT_TILES
GRID_W = 64
NORM_EPS = 1e-6
LOG2_E = math.log2(math.e)
HEAD_DIM = 64
LANES = 128
MOD_ROWS = 24
CTX_MOD_ROW = BATCH

ATTN_WIDTH = 512
ATTN_HEADS = 8
ATTN_KV_WIDTH = 128
ROPE_THETA = 10000.0

GDN_WIDTH = 256
GDN_HEADS = 4
GDN_DK = 64
GDN_QKV_WIDTH = 768
GDN_CHUNK = 64
N_CHUNKS = STREAM // GDN_CHUNK
N_LAT_CHUNKS = SEQ // GDN_CHUNK
N_CTX_CHUNKS = CTX_LEN // GDN_CHUNK
GDN_GROUP = 4

HYENA_WIDTH = 256
HYENA_EMB_BANDS = 16
HYENA_FILTER_HIDDEN = 64
HYENA_FAST_DECAY = 0.3
HYENA_SLOW_DECAY = 1.5
HYENA_DECAY_TARGET = 1e-2
HYENA_ROW_BLOCK = 512

D_FF = 2816
FF_CHUNK = 256
N_FF_CHUNKS = D_FF // FF_CHUNK
HALO = 8
FFN_ROWS = 512

COL_ATTN = (0, 768)
COL_GQKV = (768, 1536)
COL_GGATE = (1536, 1792)
COL_HY = (1792, 2560)
COL_BD = (2560, 2688)
IN_COLS = 2688

VMEM_LIMIT = 56 * 1024 * 1024


def _dot(a, b):
    return jnp.dot(a, b, preferred_element_type=F32)


def _dot_nt(a, b):
    return lax.dot_general(a, b, (((1,), (1,)), ((), ())), preferred_element_type=F32)


def _dot_tn(a, b):
    return lax.dot_general(a, b, (((0,), (0,)), ((), ())), preferred_element_type=F32)


def _sigmoid(x):
    return 1.0 / (1.0 + jnp.exp(-x))


def _silu(x):
    return x * _sigmoid(x)


def _softplus(x):
    return jnp.maximum(x, 0.0) + jnp.log(1.0 + jnp.exp(-jnp.abs(x)))


def _seg_sum(x, seg):
    hi = x.astype(BF16)
    lo = (x - hi.astype(F32)).astype(BF16)
    return _dot(hi, seg) + _dot(lo, seg)


def _tri_dot(tri, x):
    x1 = x.astype(BF16)
    r1 = x - x1.astype(F32)
    x2 = r1.astype(BF16)
    x3 = (r1 - x2.astype(F32)).astype(BF16)
    return _dot(tri, x1) + _dot(tri, x2) + _dot(tri, x3)


def _dot_split(x, sel, terms):
    out = None
    for _ in range(terms):
        piece = x.astype(BF16)
        x = x - piece.astype(F32)
        out = _dot(piece, sel) if out is None else out + _dot(piece, sel)
    return out


def _params(*sem):
    return pltpu.CompilerParams(dimension_semantics=sem, vmem_limit_bytes=VMEM_LIMIT)


def _const_spec(shape):
    nd = len(shape)
    return pl.BlockSpec(shape, lambda *_: (0,) * nd, pipeline_mode=pl.Buffered(1))


def _ada_kernel(c_ref, w_ref, b_ref, o_ref):
    s = _silu(c_ref[...]).astype(BF16)
    o_ref[...] = _dot(s, w_ref[...].astype(BF16)) + b_ref[...]


def _ada_call(cvec, w_ada, b_ada):
    nj = 6
    return pl.pallas_call(
        _ada_kernel,
        grid=(DEPTH, nj),
        in_specs=[
            pl.BlockSpec((MOD_ROWS, D_MODEL), lambda i, j: (0, 0)),
            pl.BlockSpec((None, D_MODEL, D_MODEL), lambda i, j: (i, 0, j)),
            pl.BlockSpec((None, 1, D_MODEL), lambda i, j: (i, 0, j)),
        ],
        out_specs=pl.BlockSpec((None, MOD_ROWS, D_MODEL), lambda i, j: (i, 0, j)),
        out_shape=jax.ShapeDtypeStruct((DEPTH, MOD_ROWS, 6 * D_MODEL), F32),
        compiler_params=_params("arbitrary", "arbitrary"),
        name="adaln",
    )(cvec, w_ada, b_ada.reshape(DEPTH, 1, 6 * D_MODEL))


def _norm_mod(x, gain, shift, scale):
    ms = jnp.mean(x * x, axis=-1, keepdims=True)
    return (x * lax.rsqrt(ms + NORM_EPS) * gain) * (1.0 + scale) + shift


def _in_kernel(xl_ref, xc_ref, m_ref, g_ref, w_ref, cos_ref, sin_ref, qg_ref, kg_ref, seg_ref,
               q_ref, kt_ref, v_ref, ogq_ref, ogg_ref, ohy_ref, obd_ref):
    seg = seg_ref[...]

    def project(x, rows):
        u = _norm_mod(x, g_ref[...], m_ref[0:1, :], m_ref[1:2, :]).astype(BF16)
        ogq_ref[0:rows, :] = _dot(u, w_ref[:, COL_GQKV[0]:COL_GQKV[1]]).astype(BF16)
        ogg_ref[0:rows, :] = _dot(u, w_ref[:, COL_GGATE[0]:COL_GGATE[1]]).astype(BF16)
        ohy_ref[0:rows, :] = _dot(u, w_ref[:, COL_HY[0]:COL_HY[1]]).astype(BF16)
        obd_ref[0:rows, :] = _dot(u, w_ref[:, COL_BD[0]:COL_BD[1]])

        a = _dot(u, w_ref[:, COL_ATTN[0]:COL_ATTN[1]])
        lane = lax.broadcasted_iota(jnp.int32, (rows, LANES), 1)
        first = (lane % 32) < 16
        low = lane < HEAD_DIM
        cosv = cos_ref[0:rows, :]
        sinv = sin_ref[0:rows, :]

        def norm_rope(x, gain):
            ms = _seg_sum(x * x, seg) * (1.0 / HEAD_DIM)
            y = x * lax.rsqrt(ms + NORM_EPS) * gain
            partner = jnp.where(first, pltpu.roll(y, LANES - 16, 1), pltpu.roll(y, 16, 1))
            return y * cosv + partner * sinv

        for j in range(ATTN_HEADS // 2):
            r = norm_rope(a[:, j * LANES:(j + 1) * LANES], qg_ref[...]) * (HEAD_DIM ** -0.5 * LOG2_E)
            swapped = pltpu.roll(r, HEAD_DIM, 1)
            if j // 2 == 0:
                qa = jnp.where(low, r, 0.0)
                qb = jnp.where(low, swapped, 0.0)
            else:
                qa = jnp.where(low, 0.0, swapped)
                qb = jnp.where(low, 0.0, r)
            q_ref[0:rows, (2 * j) * LANES:(2 * j + 1) * LANES] = qa.astype(BF16)
            q_ref[0:rows, (2 * j + 1) * LANES:(2 * j + 2) * LANES] = qb.astype(BF16)
        rk = norm_rope(a[:, ATTN_WIDTH:ATTN_WIDTH + ATTN_KV_WIDTH], kg_ref[...])
        kt_ref[:, 0:rows] = rk.T.astype(BF16)
        v_ref[0:rows, :] = a[:, ATTN_WIDTH + ATTN_KV_WIDTH:].astype(BF16)

    t = pl.program_id(1)

    @pl.when(t < SEQ // IN_ROWS)
    def _():
        project(xl_ref[...], IN_ROWS)

    @pl.when(t == SEQ // IN_ROWS)
    def _():
        project(xc_ref[...], CTX_LEN)
        pad = IN_ROWS - CTX_LEN
        for ref in (q_ref, v_ref, ogq_ref, ogg_ref, ohy_ref, obd_ref):
            ref[CTX_LEN:IN_ROWS, :] = jnp.zeros((pad, ref.shape[1]), ref.dtype)
        kt_ref[:, CTX_LEN:IN_ROWS] = jnp.zeros((LANES, pad), BF16)


def _in_call(h_lat, h_ctx, mod, gain, w_cat, cos_t, sin_t, qg, kg, seg):
    n_lat = SEQ // IN_ROWS

    def tile(width):
        return pl.BlockSpec((None, IN_ROWS, width), lambda b, t: (b, t, 0))

    def padded(width, dtype):
        return jax.ShapeDtypeStruct((BATCH, STREAM_PAD, width), dtype)
    return pl.pallas_call(
        _in_kernel,
        grid=(BATCH, n_lat + 1),
        in_specs=[
            pl.BlockSpec((None, IN_ROWS, D_MODEL), lambda b, t: (b, jnp.minimum(t, n_lat - 1), 0)),
            pl.BlockSpec((None, CTX_LEN, D_MODEL), lambda b, t: (b, 0, 0)),
            pl.BlockSpec((None, 6, D_MODEL), lambda b, t: (jnp.where(t == n_lat, CTX_MOD_ROW, b), 0, 0)),
            _const_spec((1, D_MODEL)),
            _const_spec((D_MODEL, IN_COLS)),
            pl.BlockSpec((IN_ROWS, LANES), lambda b, t: (t, 0)),
            pl.BlockSpec((IN_ROWS, LANES), lambda b, t: (t, 0)),
            _const_spec((1, LANES)),
            _const_spec((1, LANES)),
            _const_spec((LANES, LANES)),
        ],
        out_specs=[
            tile(ATTN_HEADS * LANES),
            pl.BlockSpec((None, LANES, IN_ROWS), lambda b, t: (b, 0, t)),
            tile(LANES), tile(768), tile(256), tile(768), tile(LANES)],
        out_shape=[
            padded(ATTN_HEADS * LANES, BF16),
            jax.ShapeDtypeStruct((BATCH, LANES, STREAM_PAD), BF16),
            padded(LANES, BF16),
            padded(768, BF16), padded(256, BF16), padded(768, BF16), padded(LANES, F32),
        ],
        compiler_params=_params("arbitrary", "arbitrary"),
        name="in_proj",
    )(h_lat, h_ctx, mod, gain, w_cat, cos_t, sin_t, qg, kg, seg)


def _attn_kernel(q_ref, kt_ref, v_ref, o_ref, *, with_ctx):
    low = lax.broadcasted_iota(jnp.int32, (TILE, LANES), 1) < HEAD_DIM

    def run(k0, k1):
        kt = kt_ref[:, k0:k1]
        v = v_ref[k0:k1, :]
        s_next = _dot(q_ref[:, 0:LANES], kt)
        outs = []
        for hq in range(ATTN_HEADS):
            s = s_next
            if hq + 1 < ATTN_HEADS:
                s_next = _dot(q_ref[:, (hq + 1) * LANES:(hq + 2) * LANES], kt)
            p = jnp.exp2(s - jnp.max(s, axis=-1, keepdims=True))
            den = jnp.sum(p, axis=-1, keepdims=True)
            outs.append(_dot(p.astype(BF16), v) / den)
            if hq % 2 == 1:
                j = hq // 2
                if j // 2 == 0:
                    blk = jnp.where(low, outs[0], pltpu.roll(outs[1], HEAD_DIM, 1))
                else:
                    blk = jnp.where(low, pltpu.roll(outs[0], HEAD_DIM, 1), outs[1])
                o_ref[:, j * LANES:(j + 1) * LANES] = blk.astype(BF16)
                outs = []

    if with_ctx:
        t = pl.program_id(1)

        @pl.when(t < CTX_TILE)
        def _():
            run(0, STREAM)

        @pl.when(t == CTX_TILE)
        def _():
            run(SEQ, STREAM)
    else:
        run(0, STREAM)


def _attn_call(q_pad, kt, v, with_ctx):
    nt = N_TILES if with_ctx else N_LAT_TILES
    return pl.pallas_call(
        functools.partial(_attn_kernel, with_ctx=with_ctx),
        grid=(BATCH, nt),
        in_specs=[
            pl.BlockSpec((None, TILE, ATTN_HEADS * LANES), lambda b, t: (b, t, 0)),
            pl.BlockSpec((None, LANES, STREAM), lambda b, t: (b, 0, 0)),
            pl.BlockSpec((None, STREAM, LANES), lambda b, t: (b, 0, 0)),
        ],
        out_specs=pl.BlockSpec((None, TILE, ATTN_WIDTH), lambda b, t: (b, t, 0)),
        out_shape=jax.ShapeDtypeStruct((BATCH, STREAM, ATTN_WIDTH), BF16),
        compiler_params=_params("arbitrary", "arbitrary"),
        name="attention",
    )(q_pad, kt, v)


def _gdn_kernel(gq_ref, gg_ref, bd_ref, cw_ref, alog_ref, dtb_ref, ng_ref, seg_ref, tril_ref, triu_ref, ones_ref,
                eg_ref, eb_ref, o_ref, q_s, k_s, v_s, g_s, b_s, u_s, w_s, qk_s, qd_s, kd_s, ge_s, st_s, o_s):
    C = GDN_CHUNK
    seg = seg_ref[...]

    def prep_tile(i, carry):
        r0 = pl.multiple_of(i * TILE, TILE)
        x = gq_ref[pl.ds(r0, TILE), :].astype(F32)
        p0 = pl.multiple_of(jnp.maximum(r0 - 16, 0), 16)
        n0 = pl.multiple_of(jnp.minimum(r0 + TILE, STREAM - 16), 16)
        first_of_segment = jnp.logical_or(i == 0, i == CTX_TILE)
        last_of_segment = jnp.logical_or(i == N_LAT_TILES - 1, i == CTX_TILE)
        xp = gq_ref[pl.ds(p0, 16), :].astype(F32)[15:16, :] * jnp.where(first_of_segment, 0.0, 1.0)
        xn = gq_ref[pl.ds(n0, 16), :].astype(F32)[0:1, :] * jnp.where(last_of_segment, 0.0, 1.0)
        row = lax.broadcasted_iota(jnp.int32, (TILE, GDN_QKV_WIDTH), 0)
        xdn = jnp.where(row == 0, xp, pltpu.roll(x, 1, 0))
        xup = jnp.where(row == TILE - 1, xn, pltpu.roll(x, TILE - 1, 0))
        c = _silu(cw_ref[0:1, :] * xdn + cw_ref[1:2, :] * x + cw_ref[2:3, :] * xup)
        q = c[:, 0:256]
        k = c[:, 256:512]
        q = q * lax.rsqrt(_seg_sum(q * q, seg) + NORM_EPS)
        k = k * lax.rsqrt(_seg_sum(k * k, seg) + NORM_EPS)
        q_s[pl.ds(r0, TILE), :] = q * (GDN_DK ** -0.5)
        k_s[pl.ds(r0, TILE), :] = k
        v_s[pl.ds(r0, TILE), :] = c[:, 512:768]
        bd = bd_ref[pl.ds(r0, TILE), :]
        b_s[pl.ds(r0, TILE), :] = _sigmoid(bd)
        g_s[pl.ds(r0, TILE), :] = -jnp.exp(alog_ref[...]) * _softplus(bd + dtb_ref[...])
        return carry

    lax.fori_loop(0, N_TILES, prep_tile, 0)

    G = GDN_GROUP
    R = G * C
    ri = lax.broadcasted_iota(jnp.int32, (C, GDN_WIDTH), 0)
    cj = lax.broadcasted_iota(jnp.int32, (C, GDN_WIDTH), 1) % C
    lane = lax.broadcasted_iota(jnp.int32, (R, LANES), 1)
    eye4 = jnp.where(ri == cj, 1.0, 0.0)
    same16 = (ri // 16) == (cj // 16)
    off32 = jnp.logical_and((ri // 32) == (cj // 32), jnp.logical_not(same16))
    off64 = (ri // 32) != (cj // 32)
    incl = (ri >= cj, ri <= cj)
    strict = (ri > cj, ri < cj)

    def block_diag(z):
        zb = z.astype(BF16)
        return jnp.concatenate([zb, zb, zb, zb], axis=0) * seg

    def phase1(it, carry):
        c0 = it * G
        r0 = pl.multiple_of(c0 * C, R)
        kc = k_s[pl.ds(r0, R), :]
        qc = q_s[pl.ds(r0, R), :]
        vc = v_s[pl.ds(r0, R), :]
        gt = g_s[pl.ds(r0, R), :]
        bt = b_s[pl.ds(r0, R), :]
        g_fwd = _tri_dot(tril_ref[...], gt)
        g_bwd = _tri_dot(triu_ref[...], gt)
        gc = jnp.where(lane < 8 + GDN_HEADS, g_fwd, g_bwd)
        gx = _dot_split(gc, eg_ref[...], 3)
        bx = _dot_split(bt, eb_ref[...], 2)
        gtx = jnp.concatenate([
            jnp.concatenate([jnp.broadcast_to(gx[g * C + C - 1:g * C + C, 0:GDN_WIDTH], (C, GDN_WIDTH)),
                             jnp.broadcast_to(gx[g * C:g * C + 1, GDN_WIDTH:], (C, GDN_WIDTH))], axis=1)
            for g in range(G)], axis=0)
        e_g = jnp.exp(gx)
        e_gk = jnp.exp(gtx - gx)
        eye_rows = jnp.concatenate([jnp.concatenate([eye4, eye4], axis=1)] * G, axis=0)
        grow = _tri_dot(ones_ref[...], jnp.where(eye_rows > 0.0, gx, 0.0))
        kb = kc.astype(BF16)
        qb = qc.astype(BF16)
        kk, qk0 = [], []
        for g in range(G):
            rs = slice(g * C, (g + 1) * C)
            bdk = jnp.concatenate([kb[rs]] * 4, axis=0) * seg
            kk.append(_dot_nt(kb[rs], bdk))
            qk0.append(_dot_nt(qb[rs], bdk))
        probs = [(g, d) for g in range(G) for d in range(2)]

        def part(arr, g, d):
            return arr[g * C:(g + 1) * C, d * GDN_WIDTH:(d + 1) * GDN_WIDTH]

        decay = [jnp.where(incl[d], jnp.exp(part(gx, g, d) - part(grow, g, d)), 0.0) for g, d in probs]
        a = [jnp.where(strict[d], part(bx, g, d) * kk[g] * dec, 0.0) for (g, d), dec in zip(probs, decay)]
        x = [jnp.where(same16, -ai, 0.0) for ai in a]
        inv = [eye4 + xi for xi in x]
        y = [_dot(xi.astype(BF16), block_diag(xi)) for xi in x]
        for stage in range(3):
            if stage < 2:
                prod = [_dot(yi.astype(BF16), jnp.concatenate([block_diag(ii), block_diag(yi)], axis=1))
                        for yi, ii in zip(y, inv)]
                inv = [ii + pi[:, 0:GDN_WIDTH] for ii, pi in zip(inv, prod)]
                y = [pi[:, GDN_WIDTH:] for pi in prod]
            else:
                inv = [ii + _dot(yi.astype(BF16), block_diag(ii)) for yi, ii in zip(y, inv)]
        for off in (off32, off64):
            t = [_dot(ii.astype(BF16), block_diag(jnp.where(off, ai, 0.0))) for ii, ai in zip(inv, a)]
            inv = [ii - _dot(ti.astype(BF16), block_diag(ii)) for ii, ti in zip(inv, t)]
        for (g, d), ii, dec in zip(probs, inv, decay):
            rs = slice(g * C, (g + 1) * C)
            bxi = part(bx, g, d)
            egi = part(e_g, g, d)
            rhs = jnp.concatenate([block_diag(vc[rs] * bxi), block_diag(kc[rs] * (bxi * egi))], axis=1)
            r = _dot(ii.astype(BF16), rhs)
            u_s[c0 + g, d] = r[:, 0:GDN_WIDTH]
            w_s[c0 + g, d] = r[:, GDN_WIDTH:].astype(BF16)
            qk_s[c0 + g, d] = (qk0[g] * dec).astype(BF16)
            qd_s[c0 + g, d] = (qc[rs] * egi).astype(BF16)
            kd_s[c0 + g, d] = (kc[rs] * part(e_gk, g, d)).astype(BF16)
            ge_s[c0 + g, d] = jnp.exp(part(gtx, g, d)[0:8, :])
        return carry

    lax.fori_loop(0, N_CHUNKS // G, phase1, 0)

    st_s[...] = jnp.zeros(st_s.shape, F32)
    o_s[...] = jnp.zeros(o_s.shape, F32)
    bi = lax.broadcasted_iota(jnp.int32, (GDN_WIDTH, GDN_WIDTH), 0) // C
    bj = lax.broadcasted_iota(jnp.int32, (GDN_WIDTH, GDN_WIDTH), 1) // C
    on_diag = bi == bj

    def phase2(s, carry):
        cs = (jnp.where(s < N_CTX_CHUNKS, s + N_LAT_CHUNKS, s - N_CTX_CHUNKS), N_CHUNKS - 1 - s)
        st = [st_s[d] for d in range(2)]
        stb = [x.astype(BF16) for x in st]
        r1 = [_dot(jnp.concatenate([w_s[cs[d], d], qd_s[cs[d], d]], axis=0), stb[d]) for d in range(2)]
        vn = [u_s[cs[d], d] - r1[d][0:C, :] for d in range(2)]
        vnb = [x.astype(BF16) for x in vn]
        o = [r1[d][C:2 * C, :] + _dot(qk_s[cs[d], d], jnp.concatenate([vnb[d]] * 4, axis=0) * seg)
             for d in range(2)]
        upd = [_dot_tn(kd_s[cs[d], d], vnb[d]) for d in range(2)]
        for d in range(2):
            st_s[d] = st[d] * ge_s[cs[d], d][0:1, :] + jnp.where(on_diag, upd[d], 0.0)
            r0 = pl.multiple_of(cs[d] * C, C)
            o_s[pl.ds(r0, C), :] += o[d]
        return carry

    lax.fori_loop(0, N_CHUNKS, phase2, 0)

    def finish_tile(i, carry):
        r0 = pl.multiple_of(i * TILE, TILE)
        o = o_s[pl.ds(r0, TILE), :]
        ms = _seg_sum(o * o, seg) * (1.0 / GDN_DK)
        y = o * lax.rsqrt(ms + NORM_EPS) * ng_ref[...]
        o_ref[pl.ds(r0, TILE), :] = (y * _silu(gg_ref[pl.ds(r0, TILE), :].astype(F32))).astype(BF16)
        return carry

    lax.fori_loop(0, N_TILES, finish_tile, 0)


def _gdn_call(gqkv, ggate, bd, conv_w, alog_row, dtb_row, ng_row, seg, tril, triu, ones, expand_g, expand_b):
    def per_batch(width):
        return pl.BlockSpec((None, STREAM, width), lambda b: (b, 0, 0))
    rows = GDN_GROUP * GDN_CHUNK
    packed = (N_CHUNKS, 2, GDN_CHUNK, GDN_WIDTH)
    return pl.pallas_call(
        _gdn_kernel,
        grid=(BATCH,),
        in_specs=[
            per_batch(768), per_batch(256), per_batch(LANES),
            _const_spec((3, 768)), _const_spec((1, LANES)), _const_spec((1, LANES)), _const_spec((1, GDN_WIDTH)),
            _const_spec((GDN_WIDTH, GDN_WIDTH)), _const_spec((rows, rows)), _const_spec((rows, rows)),
            _const_spec((rows, rows)), _const_spec((LANES, 2 * GDN_WIDTH)), _const_spec((LANES, 2 * GDN_WIDTH)),
        ],
        out_specs=pl.BlockSpec((None, STREAM, GDN_WIDTH), lambda b: (b, 0, 0)),
        out_shape=jax.ShapeDtypeStruct((BATCH, STREAM, GDN_WIDTH), BF16),
        scratch_shapes=[
            pltpu.VMEM((STREAM, GDN_WIDTH), F32),
            pltpu.VMEM((STREAM, GDN_WIDTH), F32),
            pltpu.VMEM((STREAM, GDN_WIDTH), F32),
            pltpu.VMEM((STREAM, LANES), F32),
            pltpu.VMEM((STREAM, LANES), F32),
            pltpu.VMEM(packed, F32),
            pltpu.VMEM(packed, BF16),
            pltpu.VMEM(packed, BF16),
            pltpu.VMEM(packed, BF16),
            pltpu.VMEM(packed, BF16),
            pltpu.VMEM((N_CHUNKS, 2, 8, GDN_WIDTH), F32),
            pltpu.VMEM((2, GDN_WIDTH, GDN_WIDTH), F32),
            pltpu.VMEM((STREAM, GDN_WIDTH), F32),
        ],
        compiler_params=_params("arbitrary"),
        name="gdn",
    )(gqkv, ggate, bd, conv_w, alog_row, dtb_row, ng_row, seg, tril, triu, ones, expand_g, expand_b)


def _hyena_filter_kernel(feat_ref, w1_ref, b1_ref, w2_ref, b2_ref, w3_ref, fr_ref, win_ref, fc_ref,
                         p_ref, q_ref, r_ref, *, n):
    fr = fr_ref[...]
    hid = jnp.sin(fr * (_dot(feat_ref[...].astype(BF16), w1_ref[...]) + b1_ref[...]))
    hid = jnp.sin(fr * (_dot(hid.astype(BF16), w2_ref[...]) + b2_ref[...]))
    filt = _dot(hid.astype(BF16), w3_ref[...])
    win = win_ref[...]
    row = lax.broadcasted_iota(jnp.int32, (n, HYENA_WIDTH), 0)
    h_fwd = filt[:, 0:HYENA_WIDTH] * win
    h_bwd = jnp.where(row == 0, 0.0, filt[:, HYENA_WIDTH:] * win)
    hcat = jnp.concatenate([h_fwd, h_bwd], axis=1).astype(BF16)
    h_re = _dot(fc_ref[:, 0:n], hcat)
    h_im = _dot(fc_ref[:, n:2 * n], hcat)
    sgn = jnp.where((row & 1) == 0, 1.0, -1.0)
    nyq = jnp.sum(sgn * (h_fwd + h_bwd), axis=0, keepdims=True)
    re = h_re[:, 0:HYENA_WIDTH] + h_re[:, HYENA_WIDTH:]
    p_ref[...] = re
    q_ref[...] = h_im[:, 0:HYENA_WIDTH] - h_im[:, HYENA_WIDTH:]
    r_ref[...] = jnp.where(row == 0, nyq, re)


def _hyena_filter_call(n, feats, w1, b1, w2, b2, w3, fr, win, fcat):
    shapes = [(n, LANES), (LANES, HYENA_FILTER_HIDDEN), (1, HYENA_FILTER_HIDDEN),
              (HYENA_FILTER_HIDDEN, HYENA_FILTER_HIDDEN), (1, HYENA_FILTER_HIDDEN),
              (HYENA_FILTER_HIDDEN, 2 * HYENA_WIDTH), (1, HYENA_FILTER_HIDDEN), (n, HYENA_WIDTH), (n, 2 * n)]
    out = jax.ShapeDtypeStruct((n, HYENA_WIDTH), F32)
    return pl.pallas_call(
        functools.partial(_hyena_filter_kernel, n=n),
        grid=(1,),
        in_specs=[_const_spec(s) for s in shapes],
        out_specs=[pl.BlockSpec((n, HYENA_WIDTH), lambda i: (0, 0))] * 3,
        out_shape=[out, out, out],
        compiler_params=_params("arbitrary"),
        name=f"hyena_filter_{n}",
    )(feats, w1, b1, w2, b2, w3, fr, win, fcat)


def _hyena_kernel(x_ref, cw_ref, bias_ref, fc_ref, p_ref, q_ref, r_ref, o_ref, w_s, x0_s, z_s, *, n):
    rows = min(n, TILE)
    n_tiles = n // rows
    inv_n = 1.0 / (2 * n)

    def conv_tile(i, carry):
        r0 = pl.multiple_of(i * rows, rows)
        x = x_ref[pl.ds(r0, rows), :].astype(F32)
        p0 = pl.multiple_of(jnp.maximum(r0 - 16, 0), 16)
        n0 = pl.multiple_of(jnp.minimum(r0 + rows, n - 16), 16)
        xp = x_ref[pl.ds(p0, 16), :].astype(F32)[15:16, :] * jnp.where(i == 0, 0.0, 1.0)
        xn = x_ref[pl.ds(n0, 16), :].astype(F32)[0:1, :] * jnp.where(i == n_tiles - 1, 0.0, 1.0)
        row = lax.broadcasted_iota(jnp.int32, (rows, 3 * HYENA_WIDTH), 0)
        xdn = jnp.where(row == 0, xp, pltpu.roll(x, 1, 0))
        xup = jnp.where(row == rows - 1, xn, pltpu.roll(x, rows - 1, 0))
        c = cw_ref[0:1, :] * xdn + cw_ref[1:2, :] * x + cw_ref[2:3, :] * xup
        x0_s[pl.ds(r0, rows), :] = c[:, 0:HYENA_WIDTH]
        w_s[pl.ds(r0, rows), :] = c[:, 2 * HYENA_WIDTH:] * c[:, HYENA_WIDTH:2 * HYENA_WIDTH]
        return carry

    lax.fori_loop(0, n_tiles, conv_tile, 0)

    wb = w_s[...].astype(BF16)
    rb = min(n, HYENA_ROW_BLOCK)
    row = lax.broadcasted_iota(jnp.int32, (rb, HYENA_WIDTH), 0)
    sgn = jnp.where((row & 1) == 0, 1.0, -1.0)
    u_nyq = jnp.sum(sgn * w_s[0:rb, :], axis=0, keepdims=True)
    for r0 in range(rb, n, rb):
        u_nyq = u_nyq + jnp.sum(sgn * w_s[r0:r0 + rb, :], axis=0, keepdims=True)
    z_nyq = None
    for r0 in range(0, n, rb):
        rows = slice(r0, r0 + rb)
        u_re = _dot(fc_ref[rows, 0:n], wb)
        u_im = _dot(fc_ref[rows, n:2 * n], wb)
        fscale = 2.0 * inv_n
        if r0 == 0:
            u_im = jnp.where(row == 0, u_nyq, u_im)
            fscale = jnp.where(row == 0, inv_n, 2.0 * inv_n)
        z_re = u_re * p_ref[rows, :] - u_im * q_ref[rows, :]
        z_im = u_re * q_ref[rows, :] + u_im * r_ref[rows, :]
        if r0 == 0:
            z_nyq = z_im[0:1, :] * inv_n
        z_s[r0:r0 + rb, :] = (z_re * fscale).astype(BF16)
        z_s[n + r0:n + r0 + rb, :] = (z_im * fscale).astype(BF16)
    for r0 in range(0, n, rb):
        rows = slice(r0, r0 + rb)
        y = _dot(fc_ref[rows, :], z_s[...]) + sgn * z_nyq + w_s[rows, :] * bias_ref[...]
        o_ref[rows, :] = (y * x0_s[rows, :]).astype(BF16)


def _hyena_call(n, tile_index, hy, conv_w, bias, fcat, hp, hq, hr):
    return pl.pallas_call(
        functools.partial(_hyena_kernel, n=n),
        grid=(BATCH,),
        in_specs=[
            pl.BlockSpec((None, n, 3 * HYENA_WIDTH), lambda b: (b, tile_index, 0)),
            _const_spec((3, 3 * HYENA_WIDTH)),
            _const_spec((1, HYENA_WIDTH)),
            _const_spec((n, 2 * n)),
            _const_spec((n, HYENA_WIDTH)), _const_spec((n, HYENA_WIDTH)), _const_spec((n, HYENA_WIDTH)),
        ],
        out_specs=pl.BlockSpec((None, n, HYENA_WIDTH), lambda b: (b, 0, 0)),
        out_shape=jax.ShapeDtypeStruct((BATCH, n, HYENA_WIDTH), BF16),
        scratch_shapes=[
            pltpu.VMEM((n, HYENA_WIDTH), F32),
            pltpu.VMEM((n, HYENA_WIDTH), F32),
            pltpu.VMEM((2 * n, HYENA_WIDTH), BF16),
        ],
        compiler_params=_params("arbitrary"),
        name=f"hyena_{n}",
    )(hy, conv_w, bias, fcat, hp, hq, hr)


def _block_kernel(h_ref, hp_ref, hn_ref, a_ref, ap_ref, an_ref, d_ref, dp_ref, dn_ref, y_ref, yp_ref, yn_ref,
                  m_ref, g_ref, wo_ref, wu_ref, cw_ref, wd_ref, fg_ref, o_ref, act_s, *, rows, final):
    t = pl.program_id(1)
    ext = rows + 2 * HALO

    def mixed(h, a, d, y):
        acc = _dot(a, wo_ref[0:512, :]) + _dot(d, wo_ref[512:768, :]) + _dot(y, wo_ref[768:1024, :])
        return h + m_ref[2:3, :] * acc

    wide = 2 * HALO
    h1_all = mixed(
        jnp.concatenate([hp_ref[...], hp_ref[...], h_ref[...], hn_ref[...], hn_ref[...]], axis=0),
        jnp.concatenate([ap_ref[...], a_ref[...], an_ref[...]], axis=0),
        jnp.concatenate([dp_ref[...], d_ref[...], dn_ref[...]], axis=0),
        jnp.concatenate([yp_ref[...], y_ref[...], yn_ref[...]], axis=0))
    h1_top = h1_all[HALO:wide, :]
    h1 = h1_all[wide:wide + rows, :]
    h1_bottom = h1_all[wide + rows:wide + rows + HALO, :]
    gain, shift, scale = g_ref[...], m_ref[3:4, :], m_ref[4:5, :]
    u = jnp.concatenate([
        jnp.where(t == 0, 0.0, _norm_mod(h1_top, gain, shift, scale)).astype(BF16),
        _norm_mod(h1, gain, shift, scale).astype(BF16),
        jnp.where(t == pl.num_programs(1) - 1, 0.0, _norm_mod(h1_bottom, gain, shift, scale)).astype(BF16)],
        axis=0)
    def up_conv(col):
        cols = slice(col, col + FF_CHUNK)
        p = _dot(u, wu_ref[:, cols])
        c = (cw_ref[0:1, cols] * pltpu.roll(p, 1, 0) + cw_ref[1:2, cols] * p
             + cw_ref[2:3, cols] * pltpu.roll(p, ext - 1, 0))
        return c[HALO:HALO + rows, :]

    for j in range(N_FF_CHUNKS):
        gate = up_conv(j * FF_CHUNK)
        value = up_conv(D_FF + j * FF_CHUNK)
        act_s[:, j * FF_CHUNK:(j + 1) * FF_CHUNK] = (_silu(gate) * value).astype(BF16)
    res = h1 + m_ref[5:6, :] * _dot(act_s[...], wd_ref[...])
    if final:
        ms = jnp.mean(res * res, axis=-1, keepdims=True)
        res = res * lax.rsqrt(ms + NORM_EPS) * fg_ref[...]
    o_ref[...] = res


def _block_call(h, attn, gdn, hy, mod, gain, w_out, w_up, conv_w, w_down, final_gain, *, rows, context, final):
    seg_rows = h.shape[1]
    base = SEQ if context else 0
    wide = 2 * HALO
    mod_row = (lambda b: CTX_MOD_ROW) if context else (lambda b: b)

    def main(width, offset):
        return pl.BlockSpec((None, rows, width), lambda b, t: (b, offset // rows + t, 0))

    def halos(width, block, offset, total):
        per, first, last = rows // block, offset // block, total // block - 1
        return [pl.BlockSpec((None, block, width), lambda b, t: (b, jnp.maximum(first + t * per - 1, 0), 0)),
                pl.BlockSpec((None, block, width), lambda b, t: (b, jnp.minimum(first + (t + 1) * per, last), 0))]

    return pl.pallas_call(
        functools.partial(_block_kernel, rows=rows, final=final),
        grid=(BATCH, seg_rows // rows),
        in_specs=[
            main(D_MODEL, 0), *halos(D_MODEL, HALO, 0, seg_rows),
            main(ATTN_WIDTH, base), *halos(ATTN_WIDTH, wide, base, STREAM),
            main(GDN_WIDTH, base), *halos(GDN_WIDTH, wide, base, STREAM),
            main(HYENA_WIDTH, 0), *halos(HYENA_WIDTH, wide, 0, seg_rows),
            pl.BlockSpec((None, 6, D_MODEL), lambda b, t: (mod_row(b), 0, 0)),
            _const_spec((1, D_MODEL)),
            _const_spec((D_MODEL, D_MODEL)),
            _const_spec((D_MODEL, 2 * D_FF)),
            _const_spec((3, 2 * D_FF)),
            _const_spec((D_FF, D_MODEL)),
            _const_spec((1, D_MODEL)),
        ],
        out_specs=pl.BlockSpec((None, rows, D_MODEL), lambda b, t: (b, t, 0)),
        out_shape=jax.ShapeDtypeStruct(h.shape, F32),
        scratch_shapes=[pltpu.VMEM((rows, D_FF), BF16)],
        compiler_params=_params("arbitrary", "arbitrary"),
        name="block_ctx" if context else "block",
    )(h, h, h, attn, attn, attn, gdn, gdn, gdn, hy, hy, hy, mod, gain, w_out, w_up, conv_w, w_down, final_gain)


def _block_diag_ones(width, block):
    i = jnp.arange(width)
    return (i[:, None] // block == i[None, :] // block).astype(BF16)


def _rope_tables():
    n = jnp.arange(SEQ)
    lane = jnp.arange(LANES)
    d = lane % HEAD_DIM
    comp = jnp.where((d < HEAD_DIM // 2)[None, :], (n // GRID_W)[:, None], (n % GRID_W)[:, None]).astype(F32)
    inv = ROPE_THETA ** (-(d % 16).astype(F32) / 16.0)
    ang = comp * inv[None, :]
    sign = jnp.where((d % 32) < 16, -1.0, 1.0)[None, :]
    rest = STREAM_PAD - SEQ
    cos_t = jnp.concatenate([jnp.cos(ang), jnp.ones((rest, LANES), F32)], axis=0)
    sin_t = jnp.concatenate([jnp.sin(ang) * sign, jnp.zeros((rest, LANES), F32)], axis=0)
    return cos_t, sin_t


def _dft_table(n):
    f = jnp.arange(n, dtype=jnp.int32)
    m = (f[:, None] * f[None, :]) % (2 * n)
    ang = m.astype(F32) * (math.pi / n)
    return jnp.concatenate([jnp.cos(ang), -jnp.sin(ang)], axis=1).astype(BF16)


def _hyena_static(n):
    t = jnp.linspace(0.0, 1.0, n, dtype=F32)[:, None]
    omega = 2.0 * math.pi * jnp.arange(n, dtype=F32)[:, None] / n
    bands = jnp.linspace(1e-4, HYENA_EMB_BANDS - 1, HYENA_EMB_BANDS, dtype=F32)[None, :]
    feats = jnp.concatenate([t, jnp.cos(bands * omega), -jnp.sin(bands * omega)], axis=-1)
    feats = jnp.pad(feats, ((0, 0), (0, LANES - feats.shape[1])))
    max_decay = math.log(HYENA_DECAY_TARGET) / HYENA_FAST_DECAY
    min_decay = math.log(HYENA_DECAY_TARGET) / HYENA_SLOW_DECAY
    deltas = jnp.abs(jnp.linspace(min_decay, max_decay, HYENA_WIDTH, dtype=F32))
    window = jnp.exp(-t * deltas[None, :])
    return feats, window


def kernel(x, c, ctx, c_ctx, w_ada, b_ada, norm1, norm2, w_in, w_out, q_gain, k_gain, gdn_conv, gdn_a_log,
           gdn_dt_bias, gdn_norm, hyena_conv, hyena_w1, hyena_b1, hyena_w2, hyena_b2, hyena_w3, hyena_freq,
           hyena_bias, ffn_up, ffn_conv, ffn_down, final_norm):
    assert x.shape == (BATCH, SEQ, D_MODEL) and ctx.shape == (BATCH, CTX_LEN, D_MODEL)

    h_lat, h_ctx = x, ctx
    cvec = jnp.concatenate([c, c_ctx[None, :], jnp.zeros((MOD_ROWS - BATCH - 1, D_MODEL), F32)], axis=0)
    mod_all = _ada_call(cvec, w_ada, b_ada).reshape(DEPTH, MOD_ROWS, 6, D_MODEL)

    cos_t, sin_t = _rope_tables()
    seg_head = _block_diag_ones(LANES, HEAD_DIM)
    seg_gdn = _block_diag_ones(GDN_WIDTH, GDN_DK)
    ii = jnp.arange(GDN_GROUP * GDN_CHUNK)
    same_chunk = (ii[:, None] // GDN_CHUNK) == (ii[None, :] // GDN_CHUNK)
    tril = jnp.logical_and(same_chunk, ii[:, None] >= ii[None, :]).astype(BF16)
    triu = jnp.logical_and(same_chunk, ii[:, None] <= ii[None, :]).astype(BF16)
    chunk_ones = same_chunk.astype(BF16)
    col = jnp.arange(2 * GDN_WIDTH) // GDN_DK
    expand_g = (jnp.arange(LANES)[:, None] == 8 + col[None, :]).astype(BF16)
    expand_b = (jnp.arange(LANES)[:, None] == col[None, :]).astype(BF16)
    fcat_lat = _dft_table(SEQ)
    fcat_ctx = _dft_table(CTX_LEN)
    feats_lat, win_lat = _hyena_static(SEQ)
    feats_ctx, win_ctx = _hyena_static(CTX_LEN)

    for i in range(DEPTH):
        with_ctx = i < DEPTH - 1
        mod = mod_all[i]
        wi = w_in[i]
        w_cat = jnp.concatenate(
            [wi[:, 0:768], wi[:, 768:1536], wi[:, 1536:1792], wi[:, 1808:2576], wi[:, 1792:1808],
             jnp.zeros((D_MODEL, LANES - 16), F32)], axis=1).astype(BF16)
        q_pad, kt, v, gqkv, ggate, hyp, bd = _in_call(
            h_lat, h_ctx, mod, norm1[i][None, :], w_cat, cos_t, sin_t,
            jnp.tile(q_gain[i], 2)[None, :], jnp.tile(k_gain[i], 2)[None, :], seg_head)

        attn = _attn_call(q_pad, kt, v, with_ctx)

        pad8 = jnp.zeros((8,), F32)
        alog_row = jnp.concatenate([pad8, gdn_a_log[i].reshape(-1), jnp.zeros((LANES - 16,), F32)])[None, :]
        dtb_row = jnp.concatenate([pad8, gdn_dt_bias[i].reshape(-1), jnp.zeros((LANES - 16,), F32)])[None, :]
        gdn = _gdn_call(gqkv, ggate, bd, gdn_conv[i], alog_row, dtb_row, jnp.tile(gdn_norm[i], GDN_HEADS)[None, :],
                        seg_gdn, tril, triu, chunk_ones, expand_g, expand_b)

        w1 = jnp.pad(hyena_w1[i], ((0, LANES - hyena_w1.shape[1]), (0, 0))).astype(BF16)
        filt_args = (w1, hyena_b1[i][None, :], hyena_w2[i].astype(BF16), hyena_b2[i][None, :],
                     hyena_w3[i].astype(BF16), hyena_freq[i][None, :])
        hp, hq, hr = _hyena_filter_call(SEQ, feats_lat, *filt_args, win_lat, fcat_lat)
        hy_lat = _hyena_call(SEQ, 0, hyp, hyena_conv[i], hyena_bias[i], fcat_lat, hp, hq, hr)
        if with_ctx:
            hp, hq, hr = _hyena_filter_call(CTX_LEN, feats_ctx, *filt_args, win_ctx, fcat_ctx)
            hy_ctx = _hyena_call(CTX_LEN, CTX_TILE, hyp, hyena_conv[i], hyena_bias[i], fcat_ctx, hp, hq, hr)

        weights = (mod, norm2[i][None, :], w_out[i].astype(BF16), ffn_up[i].astype(BF16), ffn_conv[i],
                   ffn_down[i].astype(BF16), final_norm[None, :])
        h_lat = _block_call(h_lat, attn, gdn, hy_lat, *weights, rows=FFN_ROWS, context=False, final=not with_ctx)
        if with_ctx:
            h_ctx = _block_call(h_ctx, attn, gdn, hy_ctx, *weights, rows=CTX_LEN, context=True, final=False)

    return h_lat
```
